```python
import math
import jax, jax.numpy as jnp
from jax import lax
import numpy as np

D_MODEL = 1024
BATCH = 8
SEQ = 2048
DEPTH = 4

MEM_LEN = 256
XA_HEADS = 4
XA_HEAD_DIM = D_MODEL // XA_HEADS
D_FF = 256 * ((8 * D_MODEL // 3 + 255) // 256)
BRANCH_W = D_MODEL // 2
S5_GROUP = 16
S5_GROUPS = BRANCH_W // S5_GROUP
S5_STATE = 64
LRU_HEADS = 8
LRU_HEAD_DIM = BRANCH_W // LRU_HEADS
LRU_CONV = 4
LRU_C = 8.0
RW_HEAD_DIM = 64
RW_HEADS = BRANCH_W // RW_HEAD_DIM
RW_DECAY_RANK = 64
RW_A_RANK = 64
RW_GATE_RANK = 128
RW_COLS = 3 * BRANCH_W + RW_DECAY_RANK + RW_A_RANK + RW_GATE_RANK
N_BRANCH = 3
IN_COLS = BRANCH_W + 2 * BRANCH_W + RW_COLS + N_BRANCH * D_MODEL
IN_SPLITS = [BRANCH_W, 2 * BRANCH_W, 3 * BRANCH_W, 3 * BRANCH_W + RW_COLS]
RW_SPLITS = [BRANCH_W, 2 * BRANCH_W, 3 * BRANCH_W, 3 * BRANCH_W + RW_DECAY_RANK,
             3 * BRANCH_W + RW_DECAY_RANK + RW_A_RANK]
ALPHA = (2 * DEPTH) ** 0.25
BETA = (8 * DEPTH) ** -0.25
LN_EPS = 1e-5
RW_GN_EPS = 64e-5

kernel_name = "hybrid_s5_rglru_rwkv7_deepnorm_trunk"

F32 = jnp.float32


def layer_norm(x, g, b):
    xf = x.astype(F32)
    mu = jnp.mean(xf, -1, keepdims=True)
    var = jnp.mean(jnp.square(xf - mu), -1, keepdims=True)
    return ((xf - mu) * lax.rsqrt(var + LN_EPS) * g + b).astype(x.dtype)


def swiglu(x, wg, wu, wd):
    return (jax.nn.silu(x @ wg) * (x @ wu)) @ wd


def _lin_op(e1, e2):
    a1, b1 = e1
    a2, b2 = e2
    return a1 * a2, a2 * b1 + b2


def token_shift(z):
    return jnp.pad(z, ((0, 0), (1, 0), (0, 0)))[:, :-1]


def s5_branch(u, lam_re, lam_im, log_dt, b_re, b_im, c_re, c_im, d, glu_w1, glu_w2):
    bsz, seq, _ = u.shape
    uf = u.astype(F32).reshape(bsz, seq, S5_GROUPS, S5_GROUP)
    lam = lax.complex(lam_re.astype(F32), lam_im.astype(F32))
    dt = jnp.exp(log_dt.astype(F32))[:, None]
    lam_bar = jnp.exp(lam * dt)
    b_mat = lax.complex(b_re.astype(F32), b_im.astype(F32))
    c_mat = lax.complex(c_re.astype(F32), c_im.astype(F32))
    b_bar = ((lam_bar - 1.0) / lam)[..., None] * b_mat
    bu = jnp.einsum('bsgc,gpc->bsgp', uf.astype(jnp.complex64), b_bar)
    a_el = jnp.broadcast_to(lam_bar, (1, seq, S5_GROUPS, S5_STATE))
    _, states = lax.associative_scan(_lin_op, (a_el, bu), axis=1)
    y = jnp.real(jnp.einsum('bsgp,gcp->bsgc', states, c_mat))
    y = y + d.astype(F32).reshape(S5_GROUPS, S5_GROUP) * uf
    y = jax.nn.gelu(y.reshape(bsz, seq, BRANCH_W)).astype(u.dtype)
    return (y @ glu_w1) * jax.nn.sigmoid(y @ glu_w2)


def causal_depthwise_conv(x, w, b):
    ch = x.shape[-1]
    out = lax.conv_general_dilated(
        x, w[:, None, :].astype(x.dtype), window_strides=(1,), padding=[(LRU_CONV - 1, 0)],
        dimension_numbers=('NWC', 'WIO', 'NWC'), feature_group_count=ch)
    return out + b


def rglru_branch(zx, zy, conv_w, conv_b, wa, ba, wx, bx, lam, w_out):
    bsz, seq, _ = zx.shape
    xc = causal_depthwise_conv(zx, conv_w, conv_b)
    xh = xc.reshape(bsz, seq, LRU_HEADS, LRU_HEAD_DIM)
    gate_r = jax.nn.sigmoid((jnp.einsum('bshi,hij->bshj', xh, wa).reshape(bsz, seq, BRANCH_W) + ba).astype(F32))
    gate_i = jax.nn.sigmoid((jnp.einsum('bshi,hij->bshj', xh, wx).reshape(bsz, seq, BRANCH_W) + bx).astype(F32))
    log_a = -LRU_C * gate_r * jax.nn.softplus(-lam.astype(F32))
    a = jnp.exp(log_a)
    mult = jnp.sqrt(-jnp.expm1(2.0 * log_a))
    b_in = mult * gate_i * xc.astype(F32)
    _, h = lax.associative_scan(_lin_op, (a, b_in), axis=1)
    y = (h * jax.nn.gelu(zy.astype(F32))).astype(zx.dtype)
    return y @ w_out


def rwkv7_scan(r, w, k, v, a, b):
    bsz, _, h, n = r.shape

    def step(state, inp):
        r_t, w_t, k_t, v_t, a_t, b_t = inp
        sa = jnp.einsum('bhvk,bhk->bhv', state, a_t)
        state = (state * w_t[:, :, None, :] + sa[..., None] * b_t[:, :, None, :]
                 + v_t[..., None] * k_t[:, :, None, :])
        return state, jnp.einsum('bhvk,bhk->bhv', state, r_t)

    xs = tuple(jnp.moveaxis(t, 1, 0) for t in (r, w, k, v, a, b))
    _, ys = lax.scan(step, jnp.zeros((bsz, h, n, n), F32), xs)
    return jnp.moveaxis(ys, 0, 1)


def rwkv7_branch(z, mu, w0, w_up, a0, a_up, g_up, k_k, k_a, r_k, ln_g, ln_b, w_out):
    bsz, seq, _ = z.shape
    z = z + mu * (token_shift(z) - z)
    r, k, v, wd, ad, gd = jnp.split(z, RW_SPLITS, axis=-1)
    w_log = -jax.nn.softplus(-(w0 + jnp.tanh(wd) @ w_up).astype(F32)) - 0.5
    decay = jnp.exp(-jnp.exp(w_log))
    a = jax.nn.sigmoid((a0 + ad @ a_up).astype(F32))
    g = jax.nn.sigmoid(gd) @ g_up

    def heads(t):
        return t.astype(F32).reshape(bsz, seq, RW_HEADS, RW_HEAD_DIM)

    kk = heads(k * k_k)
    kk = kk * lax.rsqrt(jnp.sum(kk * kk, -1, keepdims=True) + 1e-12)
    k = k.astype(F32) * (1.0 + (a - 1.0) * k_a)
    rh, kh, vh, ah, wh = heads(r), heads(k), heads(v), heads(a), heads(decay)
    y = rwkv7_scan(rh, wh, kh, vh, -kk, kk * ah)
    y_mu = jnp.mean(y, -1, keepdims=True)
    y_var = jnp.mean(jnp.square(y - y_mu), -1, keepdims=True)
    y = ((y - y_mu) * lax.rsqrt(y_var + RW_GN_EPS)).reshape(bsz, seq, BRANCH_W) * ln_g + ln_b
    bonus = jnp.sum(rh * kh * r_k, -1, keepdims=True) * vh
    y = (y + bonus.reshape(bsz, seq, BRANCH_W)) * g
    return y.astype(z.dtype) @ w_out


def cross_attention(x, mem, wq, wkv, wo):
    bsz, seq, _ = x.shape
    q = (x @ wq).reshape(bsz, seq, XA_HEADS, XA_HEAD_DIM)
    k, v = jnp.split(mem @ wkv, 2, axis=-1)
    k = k.reshape(bsz, -1, XA_HEADS, XA_HEAD_DIM)
    v = v.reshape(bsz, -1, XA_HEADS, XA_HEAD_DIM)
    s = jnp.einsum('bshd,bmhd->bhsm', q, k).astype(F32) * (XA_HEAD_DIM ** -0.5)
    p = jax.nn.softmax(s, axis=-1).astype(x.dtype)
    o = jnp.einsum('bhsm,bmhd->bshd', p, v).reshape(bsz, seq, D_MODEL)
    return o @ wo


def setup_inputs(seed: int = 0) -> dict:
    key = jax.random.key(seed)
    ks = iter(jax.random.split(key, 64))
    L, D, F, W = DEPTH, D_MODEL, D_FF, BRANCH_W
    G, P, C = S5_GROUPS, S5_STATE, S5_GROUP

    def nrm(shape, scale):
        return jax.random.normal(next(ks), shape, F32) * scale

    def unif(shape, lo, hi):
        return jax.random.uniform(next(ks), shape, F32, lo, hi)

    def gain(shape):
        return 1.0 + nrm(shape, 0.02)

    lru_s = unif((L, W), 0.9, 0.999) ** (1.0 / LRU_C)
    inp = {}
    inp['x'] = nrm((BATCH, SEQ, D), 1.0)
    inp['mem'] = nrm((BATCH, MEM_LEN, D), 1.0)
    inp['ffn1_wg'] = nrm((L, D, F), D ** -0.5)
    inp['ffn1_wu'] = nrm((L, D, F), D ** -0.5)
    inp['ffn1_wd'] = nrm((L, F, D), BETA * F ** -0.5)
    inp['ln1_g'] = gain((L, D))
    inp['ln1_b'] = nrm((L, D), 0.01)
    inp['w_in'] = nrm((L, D, IN_COLS), D ** -0.5)
    inp['s5_lam_re'] = -0.5 + nrm((L, G, P), 0.01)
    inp['s5_lam_im'] = math.pi * jnp.arange(P, dtype=F32) + nrm((L, G, P), 0.01)
    inp['s5_log_dt'] = unif((L, G), math.log(1e-3), math.log(1e-1))
    inp['s5_b_re'] = nrm((L, G, P, C), (2 * C) ** -0.5)
    inp['s5_b_im'] = nrm((L, G, P, C), (2 * C) ** -0.5)
    inp['s5_c_re'] = nrm((L, G, C, P), (2 * P) ** -0.5)
    inp['s5_c_im'] = nrm((L, G, C, P), (2 * P) ** -0.5)
    inp['s5_d'] = nrm((L, W), 1.0)
    inp['s5_glu_w1'] = nrm((L, W, D), W ** -0.5)
    inp['s5_glu_w2'] = nrm((L, W, D), W ** -0.5)
    inp['lru_conv_w'] = nrm((L, LRU_CONV, W), LRU_CONV ** -0.5)
    inp['lru_conv_b'] = nrm((L, W), 0.01)
    inp['lru_wa'] = nrm((L, LRU_HEADS, LRU_HEAD_DIM, LRU_HEAD_DIM), LRU_HEAD_DIM ** -0.5)
    inp['lru_ba'] = nrm((L, W), 0.01)
    inp['lru_wx'] = nrm((L, LRU_HEADS, LRU_HEAD_DIM, LRU_HEAD_DIM), LRU_HEAD_DIM ** -0.5)
    inp['lru_bx'] = nrm((L, W), 0.01)
    inp['lru_lambda'] = jnp.log(lru_s) - jnp.log1p(-lru_s)
    inp['lru_w_out'] = nrm((L, W, D), W ** -0.5)
    inp['rw_mu'] = unif((L, RW_COLS), 0.0, 1.0)
    inp['rw_w0'] = jnp.linspace(-6.0, -1.0, W, dtype=F32) + nrm((L, W), 0.1)
    inp['rw_w_up'] = nrm((L, RW_DECAY_RANK, W), 0.1 * RW_DECAY_RANK ** -0.5)
    inp['rw_a0'] = nrm((L, W), 0.1)
    inp['rw_a_up'] = nrm((L, RW_A_RANK, W), 0.1 * RW_A_RANK ** -0.5)
    inp['rw_g_up'] = nrm((L, RW_GATE_RANK, W), RW_GATE_RANK ** -0.5)
    inp['rw_k_k'] = 0.85 + nrm((L, W), 0.02)
    inp['rw_k_a'] = gain((L, W))
    inp['rw_r_k'] = nrm((L, RW_HEADS, RW_HEAD_DIM), 0.1)
    inp['rw_ln_g'] = gain((L, W))
    inp['rw_ln_b'] = nrm((L, W), 0.01)
    inp['rw_w_out'] = nrm((L, W, D), W ** -0.5)
    inp['mix_w_out'] = nrm((L, D, D), BETA * D ** -0.5)
    inp['ln2_g'] = gain((L, D))
    inp['ln2_b'] = nrm((L, D), 0.01)
    inp['xa_wq'] = nrm((L, D, D), D ** -0.5)
    inp['xa_wkv'] = nrm((L, D, 2 * D), D ** -0.5)
    inp['xa_wo'] = nrm((L, D, D), BETA * D ** -0.5)
    inp['ln3_g'] = gain((L, D))
    inp['ln3_b'] = nrm((L, D), 0.01)
    inp['ffn2_wg'] = nrm((L, D, F), D ** -0.5)
    inp['ffn2_wu'] = nrm((L, D, F), D ** -0.5)
    inp['ffn2_wd'] = nrm((L, F, D), BETA * F ** -0.5)
    inp['ln4_g'] = gain((L, D))
    inp['ln4_b'] = nrm((L, D), 0.01)
    return inp


def reference(x, mem, ffn1_wg, ffn1_wu, ffn1_wd, ln1_g, ln1_b, w_in,
              s5_lam_re, s5_lam_im, s5_log_dt, s5_b_re, s5_b_im, s5_c_re, s5_c_im, s5_d,
              s5_glu_w1, s5_glu_w2,
              lru_conv_w, lru_conv_b, lru_wa, lru_ba, lru_wx, lru_bx, lru_lambda, lru_w_out,
              rw_mu, rw_w0, rw_w_up, rw_a0, rw_a_up, rw_g_up, rw_k_k, rw_k_a, rw_r_k,
              rw_ln_g, rw_ln_b, rw_w_out,
              mix_w_out, ln2_g, ln2_b,
              xa_wq, xa_wkv, xa_wo, ln3_g, ln3_b,
              ffn2_wg, ffn2_wu, ffn2_wd, ln4_g, ln4_b):
    bsz, seq, _ = x.shape
    for l in range(DEPTH):
        x = layer_norm(ALPHA * x + 0.5 * swiglu(x, ffn1_wg[l], ffn1_wu[l], ffn1_wd[l]), ln1_g[l], ln1_b[l])
        z = x @ w_in[l]
        z_s5, z_lx, z_ly, z_rw, z_gate = jnp.split(z, IN_SPLITS, axis=-1)
        y_s5 = s5_branch(z_s5, s5_lam_re[l], s5_lam_im[l], s5_log_dt[l], s5_b_re[l], s5_b_im[l],
                         s5_c_re[l], s5_c_im[l], s5_d[l], s5_glu_w1[l], s5_glu_w2[l])
        y_lru = rglru_branch(z_lx, z_ly, lru_conv_w[l], lru_conv_b[l], lru_wa[l], lru_ba[l],
                             lru_wx[l], lru_bx[l], lru_lambda[l], lru_w_out[l])
        y_rw = rwkv7_branch(z_rw, rw_mu[l], rw_w0[l], rw_w_up[l], rw_a0[l], rw_a_up[l], rw_g_up[l],
                            rw_k_k[l], rw_k_a[l], rw_r_k[l], rw_ln_g[l], rw_ln_b[l], rw_w_out[l])
        gates = jax.nn.sigmoid(z_gate.reshape(bsz, seq, N_BRANCH, D_MODEL))
        merged = gates[:, :, 0] * y_s5 + gates[:, :, 1] * y_lru + gates[:, :, 2] * y_rw
        x = layer_norm(ALPHA * x + merged @ mix_w_out[l], ln2_g[l], ln2_b[l])
        x = layer_norm(ALPHA * x + cross_attention(x, mem, xa_wq[l], xa_wkv[l], xa_wo[l]), ln3_g[l], ln3_b[l])
        x = layer_norm(ALPHA * x + 0.5 * swiglu(x, ffn2_wg[l], ffn2_wu[l], ffn2_wd[l]), ln4_g[l], ln4_b[l])
    return x
```

```python
import functools
import math

import jax
import jax.numpy as jnp
from jax import lax
from jax.experimental import pallas as pl
from jax.experimental.pallas import tpu as pltpu

F32 = jnp.float32
BF16 = jnp.bfloat16

LN_EPS = 1e-5
RW_GN_EPS = 64e-5
LRU_C = 8.0
LRU_CONV = 4
S5_GROUP = 16
S5_STATE = 64
HEAD_DIM = 64
XA_HEADS = 4
RW_DECAY_RANK = 64
RW_A_RANK = 64
RW_GATE_RANK = 128

SUBLANES = 8
LANES = 128
VMEM_LIMIT_BYTES = 56 * 1024 * 1024


def _cparams(*sem):
    return pltpu.CompilerParams(dimension_semantics=sem, vmem_limit_bytes=VMEM_LIMIT_BYTES)


def _const_spec(shape):
    nd = len(shape)
    return pl.BlockSpec(shape, lambda *_: (0,) * nd, pipeline_mode=pl.Buffered(1))


def _layer_spec(arr, l):
    tail = arr.shape[1:]
    nd = len(tail)
    return pl.BlockSpec((None,) + tail, lambda *_: (l,) + (0,) * nd, pipeline_mode=pl.Buffered(1))


def _layer_norm(y, g, b):
    mu = jnp.mean(y, -1, keepdims=True)
    yc = y - mu
    var = jnp.mean(yc * yc, -1, keepdims=True)
    return yc * lax.rsqrt(var + LN_EPS) * g + b


def _mm(a, b):
    return jnp.dot(a, b, preferred_element_type=F32)


def _segsum(x, ones_bd):
    hi = x.astype(BF16)
    lo = (x - hi.astype(F32)).astype(BF16)
    return _mm(hi, ones_bd) + _mm(lo, ones_bd)


def _softplus(x):
    return jnp.maximum(x, 0.0) + jnp.log1p(jnp.exp(-jnp.abs(x)))


def _ffn_kernel(x_ref, wg_ref, wu_ref, wd_ref, g_ref, b_ref, o_ref, *, alpha, f_chunk):
    x = x_ref[...]
    xb = x.astype(BF16)
    d_ff = wg_ref.shape[1]
    acc = jnp.zeros(x.shape, F32)
    for c0 in range(0, d_ff, f_chunk):
        hg = _mm(xb, wg_ref[:, c0:c0 + f_chunk])
        hu = _mm(xb, wu_ref[:, c0:c0 + f_chunk])
        h = hg * jax.nn.sigmoid(hg) * hu
        acc = acc + _mm(h.astype(BF16), wd_ref[c0:c0 + f_chunk, :])
    o_ref[...] = _layer_norm(alpha * x + 0.5 * acc, g_ref[...], b_ref[...])


def _ffn_ln(x, wg, wu, wd, g, b, l, alpha, tm):
    t, d = x.shape
    row = pl.BlockSpec((tm, d), lambda i: (i, 0))
    return pl.pallas_call(
        functools.partial(_ffn_kernel, alpha=alpha, f_chunk=256),
        grid=(t // tm,),
        in_specs=[row, _layer_spec(wg, l), _layer_spec(wu, l), _layer_spec(wd, l),
                  _layer_spec(g, l), _layer_spec(b, l)],
        out_specs=row,
        out_shape=jax.ShapeDtypeStruct((t, d), F32),
        compiler_params=_cparams("parallel"),
        name="ffn_ln",
    )(x, wg, wu, wd, g, b)


def _s5_kernel(x_ref, w_ref, bq_ref, cq_ref, are_ref, aim_ref, d_ref, o_ref,
               st_ref, carry_ref, u_ref):
    rows = x_ref.shape[0]
    nq = bq_ref.shape[0]
    half = bq_ref.shape[2] // 2
    cw = bq_ref.shape[1]

    @pl.when(pl.program_id(0) == 0)
    def _():
        carry_ref[...] = jnp.zeros_like(carry_ref)

    u = _mm(x_ref[...].astype(BF16), w_ref[...])
    u_ref[...] = u
    ub = u.astype(BF16)
    for q in range(nq):
        st_ref[:, 2 * half * q:2 * half * (q + 1)] = _mm(ub[:, cw * q:cw * (q + 1)], bq_ref[q])

    def step(t, s):
        r0 = pl.multiple_of(t * SUBLANES, SUBLANES)
        parts = []
        for q in range(nq):
            re = s[:, 2 * half * q:2 * half * q + half]
            im = s[:, 2 * half * q + half:2 * half * (q + 1)]
            ar = are_ref[:, half * q:half * (q + 1)]
            ai = aim_ref[:, half * q:half * (q + 1)]
            bre = st_ref[pl.ds(r0, SUBLANES), 2 * half * q:2 * half * q + half]
            bim = st_ref[pl.ds(r0, SUBLANES), 2 * half * q + half:2 * half * (q + 1)]
            parts.append(ar * re - ai * im + bre)
            parts.append(ar * im + ai * re + bim)
        s_new = jnp.concatenate(parts, axis=1)
        st_ref[pl.ds(r0, SUBLANES), :] = s_new
        return s_new

    carry_ref[...] = lax.fori_loop(0, rows // SUBLANES, step, carry_ref[...])

    for q in range(nq):
        y = _mm(st_ref[:, 2 * half * q:2 * half * (q + 1)].astype(BF16), cq_ref[q])
        y = y + d_ref[:, cw * q:cw * (q + 1)] * u_ref[:, cw * q:cw * (q + 1)]
        o_ref[:, cw * q:cw * (q + 1)] = jax.nn.gelu(y)


def _s5_branch(x, w_s5, bq, cq, a_re, a_im, d, l, rows):
    t, dm = x.shape
    wdt = w_s5.shape[2]
    n_state = a_re.shape[2]
    return pl.pallas_call(
        _s5_kernel,
        grid=(t // rows,),
        in_specs=[pl.BlockSpec((rows, dm), lambda i: (i, 0)),
                  _layer_spec(w_s5, l), _layer_spec(bq, l), _layer_spec(cq, l),
                  _layer_spec(a_re, l), _layer_spec(a_im, l), _layer_spec(d, l)],
        out_specs=pl.BlockSpec((rows, wdt), lambda i: (i, 0)),
        out_shape=jax.ShapeDtypeStruct((t, wdt), F32),
        scratch_shapes=[pltpu.VMEM((rows, 2 * n_state), F32),
                        pltpu.VMEM((SUBLANES, 2 * n_state), F32),
                        pltpu.VMEM((rows, wdt), F32)],
        compiler_params=_cparams("arbitrary"),
        name="s5_branch",
    )(x, w_s5, bq, cq, a_re, a_im, d)


def _lru_kernel(x_ref, wlx_ref, wly_ref, cw_ref, cb_ref, wa_ref, ba_ref, wx_ref, bx_ref,
                lam_ref, o_ref, hist_ref, h_ref, a_ref, b_ref):
    rows = x_ref.shape[0]
    hist_rows = hist_ref.shape[0]

    @pl.when(pl.program_id(0) == 0)
    def _():
        hist_ref[...] = jnp.zeros_like(hist_ref)
        h_ref[...] = jnp.zeros_like(h_ref)

    xb = x_ref[...].astype(BF16)
    zx = _mm(xb, wlx_ref[...])
    zy = _mm(xb, wly_ref[...])

    ext = jnp.concatenate([hist_ref[...], zx], axis=0)
    hist_ref[...] = zx[rows - hist_rows:, :]
    xc = cb_ref[...] + cw_ref[LRU_CONV - 1:LRU_CONV, :] * zx
    for j in range(1, LRU_CONV):
        off = hist_rows - SUBLANES * j
        xc = xc + cw_ref[LRU_CONV - 1 - j:LRU_CONV - j, :] * ext[off:off + rows, :]

    xcb = xc.astype(BF16)
    gate_r = jax.nn.sigmoid(_mm(xcb, wa_ref[...]) + ba_ref[...])
    gate_i = jax.nn.sigmoid(_mm(xcb, wx_ref[...]) + bx_ref[...])
    log_a = -LRU_C * gate_r * _softplus(-lam_ref[...])
    a = jnp.exp(log_a)
    a_ref[...] = a
    b_ref[...] = jnp.sqrt(-jnp.tanh(log_a) * (1.0 + a * a)) * gate_i * xc

    def step(t, h):
        r0 = pl.multiple_of(t * SUBLANES, SUBLANES)
        h = a_ref[pl.ds(r0, SUBLANES), :] * h + b_ref[pl.ds(r0, SUBLANES), :]
        b_ref[pl.ds(r0, SUBLANES), :] = h
        return h

    h_ref[...] = lax.fori_loop(0, rows // SUBLANES, step, h_ref[...], unroll=8)
    o_ref[...] = b_ref[...] * jax.nn.gelu(zy)


def _lru_branch(x, w_lx, w_ly, conv_w, conv_b, wa_bd, ba, wx_bd, bx, lam, l, rows):
    t, dm = x.shape
    wdt = w_lx.shape[2]
    params = (w_lx, w_ly, conv_w, conv_b, wa_bd, ba, wx_bd, bx, lam)
    return pl.pallas_call(
        _lru_kernel,
        grid=(t // rows,),
        in_specs=[pl.BlockSpec((rows, dm), lambda i: (i, 0))] + [_layer_spec(p, l) for p in params],
        out_specs=pl.BlockSpec((rows, wdt), lambda i: (i, 0)),
        out_shape=jax.ShapeDtypeStruct((t, wdt), F32),
        scratch_shapes=[pltpu.VMEM(((LRU_CONV - 1) * SUBLANES, wdt), F32),
                        pltpu.VMEM((SUBLANES, wdt), F32),
                        pltpu.VMEM((rows, wdt), F32),
                        pltpu.VMEM((rows, wdt), F32)],
        compiler_params=_cparams("arbitrary"),
        name="lru_branch",
    )(x, *params)


def _rw_prep_kernel(x_ref, w_ref, mu_ref, w0_ref, wup_ref, a0_ref, aup_ref, gup_ref,
                    kk_ref, ka_ref, rk_ref, ones_ref, sel_b_ref, sel_k_ref,
                    k5_ref, v_ref, sc_ref, g_ref, bonus_ref, prev_ref):
    rows = x_ref.shape[0]
    wdt = v_ref.shape[1]
    lora0 = 3 * wdt
    lora1 = lora0 + wup_ref.shape[0]

    @pl.when(pl.program_id(0) == 0)
    def _():
        prev_ref[...] = jnp.zeros_like(prev_ref)

    z = _mm(x_ref[...].astype(BF16), w_ref[...])
    shifted = jnp.concatenate([prev_ref[...], z[:rows - SUBLANES, :]], axis=0)
    prev_ref[...] = z[rows - SUBLANES:, :]
    z = z + mu_ref[...] * (shifted - z)

    r = z[:, 0:wdt]
    k = z[:, wdt:2 * wdt]
    v = z[:, 2 * wdt:3 * wdt]
    lora = z[:, lora0:lora1]
    gd = z[:, lora1:]

    w_log = -_softplus(-(w0_ref[...] + _mm(jnp.tanh(lora).astype(BF16), wup_ref[...]))) - 0.5
    decay = jnp.exp(-jnp.exp(w_log))
    a = jax.nn.sigmoid(a0_ref[...] + _mm(lora.astype(BF16), aup_ref[...]))
    g_ref[...] = _mm(jax.nn.sigmoid(gd).astype(BF16), gup_ref[...])

    ones_bd = ones_ref[...]
    kk = k * kk_ref[...]
    kk = kk * lax.rsqrt(_segsum(kk * kk, ones_bd) + 1e-12)
    k = k * (1.0 + (a - 1.0) * ka_ref[...])
    kka = kk * a

    k5_ref[:, 0:wdt] = -kk
    k5_ref[:, wdt:2 * wdt] = decay
    k5_ref[:, 2 * wdt:3 * wdt] = kka
    k5_ref[:, 3 * wdt:4 * wdt] = k
    k5_ref[:, 4 * wdt:5 * wdt] = decay * r
    v_ref[...] = v
    sc_ref[...] = _segsum(kka * r, sel_b_ref[...]) + _segsum(k * r, sel_k_ref[...])
    bonus_ref[...] = _segsum(r * k * rk_ref[...], ones_bd) * v


def _rw_prep(x, w_rw, mu, w0, wup_pad, a0, aup_pad, gup, k_k, k_a, r_k, ones_bd, sel_b, sel_k, l, rows):
    t, dm = x.shape
    wdt = w0.shape[2]
    cols = w_rw.shape[2]
    params = (w_rw, mu, w0, wup_pad, a0, aup_pad, gup, k_k, k_a, r_k)
    row = lambda n: pl.BlockSpec((rows, n), lambda i: (i, 0))
    return pl.pallas_call(
        _rw_prep_kernel,
        grid=(t // rows,),
        in_specs=[row(dm)] + [_layer_spec(p, l) for p in params]
                 + [_const_spec(ones_bd.shape), _const_spec(sel_b.shape), _const_spec(sel_k.shape)],
        out_specs=[row(5 * wdt), row(wdt), row(LANES), row(wdt), row(wdt)],
        out_shape=[jax.ShapeDtypeStruct((t, 5 * wdt), F32), jax.ShapeDtypeStruct((t, wdt), F32),
                   jax.ShapeDtypeStruct((t, LANES), F32), jax.ShapeDtypeStruct((t, wdt), F32),
                   jax.ShapeDtypeStruct((t, wdt), F32)],
        scratch_shapes=[pltpu.VMEM((SUBLANES, cols), F32)],
        compiler_params=_cparams("arbitrary"),
        name="rw_prep",
    )(x, *params, ones_bd, sel_b, sel_k)


def _rw_scan_kernel(k5_ref, v_ref, y_ref, st_ref):
    tt = k5_ref.shape[0]
    kp_n = st_ref.shape[0]
    hd = st_ref.shape[1]
    half = LANES // 2

    @pl.when(pl.program_id(0) == 0)
    def _():
        st_ref[...] = jnp.zeros_like(st_ref)

    def step(t, carry):
        vv = v_ref[t, 0:hd, :]
        b_r = v_ref[t, hd:hd + 1, :]
        k_r = v_ref[t, hd + 1:hd + 2, :]
        sa = jnp.zeros((hd, LANES), F32)
        yp = jnp.zeros((hd, LANES), F32)
        for kp in range(kp_n):
            s_k = st_ref[kp]
            sa = sa + s_k * k5_ref[t, kp:kp + 1, :]
            yp = yp + s_k * k5_ref[t, 4 * kp_n + kp:4 * kp_n + kp + 1, :]
        sa = sa + pltpu.roll(sa, half, 1)
        yp = yp + pltpu.roll(yp, half, 1)
        for kp in range(kp_n):
            st_ref[kp] = (st_ref[kp] * k5_ref[t, kp_n + kp:kp_n + kp + 1, :]
                          + sa * k5_ref[t, 2 * kp_n + kp:2 * kp_n + kp + 1, :]
                          + vv * k5_ref[t, 3 * kp_n + kp:3 * kp_n + kp + 1, :])
        y_ref[t] = yp + sa * b_r + vv * k_r
        return carry

    lax.fori_loop(0, tt, step, 0)


def _rw_scan(k5t, v2, tt):
    s, k5_rows, _ = k5t.shape
    v_rows = v2.shape[1]
    hd = HEAD_DIM
    return pl.pallas_call(
        _rw_scan_kernel,
        grid=(s // tt,),
        in_specs=[pl.BlockSpec((tt, k5_rows, LANES), lambda i: (i, 0, 0)),
                  pl.BlockSpec((tt, v_rows, LANES), lambda i: (i, 0, 0))],
        out_specs=pl.BlockSpec((tt, hd, LANES), lambda i: (i, 0, 0)),
        out_shape=jax.ShapeDtypeStruct((s, hd, LANES), F32),
        scratch_shapes=[pltpu.VMEM((k5_rows // 5, hd, LANES), F32)],
        compiler_params=_cparams("arbitrary"),
        name="rw_scan",
    )(k5t, v2)


def _merge_kernel(x_ref, ys5_ref, ylru_ref, yrw_ref, bonus_ref, g_ref,
                  wgate_ref, glu1_ref, glu2_ref, lruo_ref, rwo_ref, mix_ref,
                  gng_ref, gnb_ref, ones_ref, lng_ref, lnb_ref, o_ref, *, alpha):
    x = x_ref[...]
    dm = x.shape[1]
    gates = jax.nn.sigmoid(_mm(x.astype(BF16), wgate_ref[...]))

    ys5 = ys5_ref[...].astype(BF16)
    y_s5 = _mm(ys5, glu1_ref[...]) * jax.nn.sigmoid(_mm(ys5, glu2_ref[...]))
    y_lru = _mm(ylru_ref[...].astype(BF16), lruo_ref[...])

    ones_bd = ones_ref[...]
    inv_n = 1.0 / HEAD_DIM
    y = yrw_ref[...]
    yc = y - _segsum(y, ones_bd) * inv_n
    var = _segsum(yc * yc, ones_bd) * inv_n
    y = yc * lax.rsqrt(var + RW_GN_EPS) * gng_ref[...] + gnb_ref[...]
    y = (y + bonus_ref[...]) * g_ref[...]
    y_rw = _mm(y.astype(BF16), rwo_ref[...])

    merged = (gates[:, 0:dm] * y_s5 + gates[:, dm:2 * dm] * y_lru + gates[:, 2 * dm:3 * dm] * y_rw)
    o_ref[...] = _layer_norm(alpha * x + _mm(merged.astype(BF16), mix_ref[...]),
                             lng_ref[...], lnb_ref[...])


def _merge(x, ys5, ylru, yrw, bonus, g, w_gate, glu1, glu2, lru_o, rw_o, mix, gn_g, gn_b, ones_bd,
           ln_g, ln_b, l, alpha, tm):
    t, dm = x.shape
    wdt = ys5.shape[1]
    row = lambda n: pl.BlockSpec((tm, n), lambda i: (i, 0))
    params = (w_gate, glu1, glu2, lru_o, rw_o, mix, gn_g, gn_b)
    return pl.pallas_call(
        functools.partial(_merge_kernel, alpha=alpha),
        grid=(t // tm,),
        in_specs=[row(dm)] + [row(wdt)] * 5 + [_layer_spec(p, l) for p in params]
                 + [_const_spec(ones_bd.shape), _layer_spec(ln_g, l), _layer_spec(ln_b, l)],
        out_specs=row(dm),
        out_shape=jax.ShapeDtypeStruct((t, dm), F32),
        compiler_params=_cparams("parallel"),
        name="merge_mix_ln",
    )(x, ys5, ylru, yrw, bonus, g, *params, ones_bd, ln_g, ln_b)


def _kv_kernel(mem_ref, w_ref, o_ref):
    o_ref[...] = _mm(mem_ref[...].astype(BF16), w_ref[...]).astype(BF16)


def _kv_proj(mem2, wkv, tn):
    rows, dm = mem2.shape
    depth, _, n = wkv.shape
    return pl.pallas_call(
        _kv_kernel,
        grid=(depth, n // tn),
        in_specs=[pl.BlockSpec((rows, dm), lambda l, j: (0, 0)),
                  pl.BlockSpec((None, dm, tn), lambda l, j: (l, 0, j))],
        out_specs=pl.BlockSpec((None, rows, tn), lambda l, j: (l, 0, j)),
        out_shape=jax.ShapeDtypeStruct((depth, rows, n), BF16),
        compiler_params=_cparams("parallel", "parallel"),
        name="xa_kv_proj",
    )(mem2, wkv)


def _xattn_kernel(x_ref, k_ref, v_ref, wq_ref, wo_ref, g_ref, b_ref, o_ref, *, alpha, heads):
    x = x_ref[...]
    dm = x.shape[1]
    hd = dm // heads
    q = _mm(x.astype(BF16), wq_ref[...]).astype(BF16)
    outs = []
    for h in range(heads):
        sl = slice(h * hd, (h + 1) * hd)
        s = lax.dot_general(q[:, sl], k_ref[:, sl], (((1,), (1,)), ((), ())),
                            preferred_element_type=F32) * (hd ** -0.5)
        e = jnp.exp(s - jnp.max(s, -1, keepdims=True))
        p = e / jnp.sum(e, -1, keepdims=True)
        outs.append(_mm(p.astype(BF16), v_ref[:, sl]))
    o = jnp.concatenate(outs, axis=1)
    o_ref[...] = _layer_norm(alpha * x + _mm(o.astype(BF16), wo_ref[...]), g_ref[...], b_ref[...])


def _xattn(x, kv, wq, wo, g, b, l, alpha, bsz, ts):
    t, dm = x.shape
    s = t // bsz
    m = kv.shape[1] // bsz
    xv = x.reshape(s, bsz * dm)
    seq = pl.BlockSpec((ts, dm), lambda bi, i: (i, bi))
    out = pl.pallas_call(
        functools.partial(_xattn_kernel, alpha=alpha, heads=XA_HEADS),
        grid=(bsz, s // ts),
        in_specs=[seq,
                  pl.BlockSpec((None, m, dm), lambda bi, i: (l, bi, 0)),
                  pl.BlockSpec((None, m, dm), lambda bi, i: (l, bi, 1)),
                  _layer_spec(wq, l), _layer_spec(wo, l), _layer_spec(g, l), _layer_spec(b, l)],
        out_specs=seq,
        out_shape=jax.ShapeDtypeStruct((s, bsz * dm), F32),
        compiler_params=_cparams("parallel", "parallel"),
        name="xattn_ln",
    )(xv, kv, kv, wq, wo, g, b)
    return out.reshape(t, dm)


def _block_diag(w):
    depth, h, n, _ = w.shape
    eye = jnp.eye(h, dtype=w.dtype)
    return (w[:, :, :, None, :] * eye[None, :, None, :, None]).reshape(depth, h * n, h * n)


def _s5_discretise(lam_re, lam_im, log_dt, b_re, b_im, c_re, c_im):
    depth, g, p = lam_re.shape
    c = b_re.shape[-1]
    gb = LANES // c
    nq = g // gb
    dt = jnp.exp(log_dt)[..., None]
    mag = jnp.exp(lam_re * dt)
    a_re = mag * jnp.cos(lam_im * dt)
    a_im = mag * jnp.sin(lam_im * dt)
    den = lam_re * lam_re + lam_im * lam_im
    co_re = ((a_re - 1.0) * lam_re + a_im * lam_im) / den
    co_im = (a_im * lam_re - (a_re - 1.0) * lam_im) / den
    bb_re = co_re[..., None] * b_re - co_im[..., None] * b_im
    bb_im = co_re[..., None] * b_im + co_im[..., None] * b_re
    eye = jnp.eye(gb, dtype=F32)

    def b_layout(w):
        w = jnp.swapaxes(w.reshape(depth, nq, gb, p, c), 3, 4)
        return (w[:, :, :, :, None, :] * eye[None, None, :, None, :, None]).reshape(depth, nq, gb * c, gb * p)

    def c_layout(w):
        w = jnp.swapaxes(w.reshape(depth, nq, gb, c, p), 3, 4)
        return (w[:, :, :, :, None, :] * eye[None, None, :, None, :, None]).reshape(depth, nq, gb * p, gb * c)

    bq = jnp.concatenate([b_layout(bb_re), b_layout(bb_im)], axis=-1).astype(BF16)
    cq = jnp.concatenate([c_layout(c_re), c_layout(-c_im)], axis=-2).astype(BF16)
    return bq, cq, a_re.reshape(depth, 1, g * p), a_im.reshape(depth, 1, g * p)


def _tile(n, target):
    return min(n, target)


def kernel(x, mem, ffn1_wg, ffn1_wu, ffn1_wd, ln1_g, ln1_b, w_in, s5_lam_re, s5_lam_im, s5_log_dt, s5_b_re, s5_b_im, s5_c_re, s5_c_im, s5_d, s5_glu_w1, s5_glu_w2, lru_conv_w, lru_conv_b, lru_wa, lru_ba, lru_wx, lru_bx, lru_lambda, lru_w_out, rw_mu, rw_w0, rw_w_up, rw_a0, rw_a_up, rw_g_up, rw_k_k, rw_k_a, rw_r_k, rw_ln_g, rw_ln_b, rw_w_out, mix_w_out, ln2_g, ln2_b, xa_wq, xa_wkv, xa_wo, ln3_g, ln3_b, ffn2_wg, ffn2_wu, ffn2_wd, ln4_g, ln4_b):
    bsz, seq, dm = x.shape
    depth = w_in.shape[0]
    wdt = s5_d.shape[1]
    heads = wdt // HEAD_DIM
    alpha = (2 * depth) ** 0.25
    assert bsz == SUBLANES, "one time step must be one aligned 8-row group"
    t = bsz * seq

    bf = lambda w: w.astype(BF16)
    vec = lambda p: p.reshape(depth, 1, -1)

    w_in_b = bf(w_in)
    w_s5, w_lx, w_ly = w_in_b[:, :, 0:wdt], w_in_b[:, :, wdt:2 * wdt], w_in_b[:, :, 2 * wdt:3 * wdt]
    rw_cols = rw_mu.shape[1]
    w_rw = w_in_b[:, :, 3 * wdt:3 * wdt + rw_cols]
    w_gate = w_in_b[:, :, 3 * wdt + rw_cols:]
    ffn1 = (bf(ffn1_wg), bf(ffn1_wu), bf(ffn1_wd), vec(ln1_g), vec(ln1_b))
    ffn2 = (bf(ffn2_wg), bf(ffn2_wu), bf(ffn2_wd), vec(ln4_g), vec(ln4_b))

    bq, cq, a_re, a_im = _s5_discretise(s5_lam_re, s5_lam_im, s5_log_dt, s5_b_re, s5_b_im, s5_c_re, s5_c_im)
    wa_bd, wx_bd = bf(_block_diag(lru_wa)), bf(_block_diag(lru_wx))

    zeros_dr = jnp.zeros((depth, RW_A_RANK, wdt), BF16)
    zeros_ar = jnp.zeros((depth, RW_DECAY_RANK, wdt), BF16)
    wup_pad = jnp.concatenate([bf(rw_w_up), zeros_dr], axis=1)
    aup_pad = jnp.concatenate([zeros_ar, bf(rw_a_up)], axis=1)
    head_of = jnp.arange(wdt) // HEAD_DIM
    ones_bd = (head_of[:, None] == head_of[None, :]).astype(BF16)
    lane = jnp.arange(LANES)
    sel_b = (head_of[:, None] == lane[None, :]).astype(BF16)
    sel_k = (head_of[:, None] + heads == lane[None, :]).astype(BF16)

    kv = _kv_proj(mem.reshape(-1, dm), bf(xa_wkv), 512)
    wq_b, wo_b = bf(xa_wq), bf(xa_wo)
    glu1, glu2, lru_o, rw_o, mix = bf(s5_glu_w1), bf(s5_glu_w2), bf(lru_w_out), bf(rw_w_out), bf(mix_w_out)

    tm = _tile(t, 512)
    kh = 2
    kp_n = HEAD_DIM // kh

    h = jnp.transpose(x, (1, 0, 2)).reshape(t, dm)
    for l in range(depth):
        h = _ffn_ln(h, *ffn1, l, alpha, tm)

        y_s5 = _s5_branch(h, w_s5, bq, cq, a_re, a_im, vec(s5_d), l, _tile(t, 512))
        y_lru = _lru_branch(h, w_lx, w_ly, lru_conv_w, vec(lru_conv_b), wa_bd, vec(lru_ba), wx_bd,
                            vec(lru_bx), vec(lru_lambda), l, _tile(t, 1024))
        k5, v, sc, g, bonus = _rw_prep(h, w_rw, vec(rw_mu), vec(rw_w0), wup_pad, vec(rw_a0), aup_pad,
                                       bf(rw_g_up), vec(rw_k_k), vec(rw_k_a), vec(rw_r_k),
                                       ones_bd, sel_b, sel_k, l, _tile(t, 512))
        k5t = k5.reshape(seq, bsz, 5, heads, kh, kp_n).transpose(0, 2, 5, 4, 1, 3).reshape(seq, 5 * kp_n, LANES)
        vt = v.reshape(seq, bsz, heads, HEAD_DIM).transpose(0, 3, 1, 2).reshape(seq, HEAD_DIM, bsz * heads)
        sct = sc.reshape(seq, bsz, LANES)[:, :, :2 * heads].reshape(seq, bsz, 2, heads)
        sct = sct.transpose(0, 2, 1, 3).reshape(seq, 2, bsz * heads)
        v2 = jnp.concatenate([vt, sct, jnp.zeros((seq, SUBLANES - 2, bsz * heads), F32)], axis=1)
        v2 = jnp.concatenate([v2, v2], axis=2)
        y = _rw_scan(k5t, v2, _tile(seq, 64))
        y_rw = y[:, :, :bsz * heads].reshape(seq, HEAD_DIM, bsz, heads).transpose(0, 2, 3, 1).reshape(t, wdt)

        h = _merge(h, y_s5, y_lru, y_rw, bonus, g, w_gate, glu1, glu2, lru_o, rw_o, mix,
                   vec(rw_ln_g), vec(rw_ln_b), ones_bd, vec(ln2_g), vec(ln2_b), l, alpha, tm)
        h = _xattn(h, kv, wq_b, wo_b, vec(ln3_g), vec(ln3_b), l, alpha, bsz, _tile(seq, 512))
        h = _ffn_ln(h, *ffn2, l, alpha, tm)
    return jnp.transpose(h.reshape(seq, bsz, dm), (1, 0, 2))
```

```python
import functools
import math

import jax
import jax.numpy as jnp
import numpy as np
from jax import lax
from jax.experimental import pallas as pl
from jax.experimental.pallas import tpu as pltpu

F32 = jnp.float32
BF16 = jnp.bfloat16

LN_EPS = 1e-5
RW_GN_EPS = 64e-5
LRU_C = 8.0
LRU_CONV = 4
S5_GROUP = 16
S5_STATE = 64
HEAD_DIM = 64
XA_HEADS = 4
RW_DECAY_RANK = 64
RW_A_RANK = 64
RW_GATE_RANK = 128

SUBLANES = 8
LANES = 128
VMEM_LIMIT_BYTES = 56 * 1024 * 1024


def _cparams(*sem):
    return pltpu.CompilerParams(dimension_semantics=sem, vmem_limit_bytes=VMEM_LIMIT_BYTES)


def _const_spec(shape):
    nd = len(shape)
    return pl.BlockSpec(shape, lambda *_: (0,) * nd, pipeline_mode=pl.Buffered(1))


def _layer_spec(arr, l):
    tail = arr.shape[1:]
    nd = len(tail)
    return pl.BlockSpec((None,) + tail, lambda *_: (l,) + (0,) * nd, pipeline_mode=pl.Buffered(1))


def _layer_norm(y, g, b):
    mu = jnp.mean(y, -1, keepdims=True)
    yc = y - mu
    var = jnp.mean(yc * yc, -1, keepdims=True)
    return yc * lax.rsqrt(var + LN_EPS) * g + b


def _mm(a, b):
    return jnp.dot(a, b, preferred_element_type=F32)


def _segsum(x, ones_bd):
    hi = x.astype(BF16)
    lo = (x - hi.astype(F32)).astype(BF16)
    return _mm(hi, ones_bd) + _mm(lo, ones_bd)


def _softplus(x):
    return jnp.maximum(x, 0.0) + jnp.log1p(jnp.exp(-jnp.abs(x)))


def _ffn_kernel(x_ref, wg_ref, wu_ref, wd_ref, g_ref, b_ref, o_ref, *, alpha, f_chunk):
    x = x_ref[...]
    xb = x.astype(BF16)
    d_ff = wg_ref.shape[1]
    acc = jnp.zeros(x.shape, F32)
    for c0 in range(0, d_ff, f_chunk):
        hg = _mm(xb, wg_ref[:, c0:c0 + f_chunk])
        hu = _mm(xb, wu_ref[:, c0:c0 + f_chunk])
        h = hg * jax.nn.sigmoid(hg) * hu
        acc = acc + _mm(h.astype(BF16), wd_ref[c0:c0 + f_chunk, :])
    o_ref[...] = _layer_norm(alpha * x + 0.5 * acc, g_ref[...], b_ref[...])


def _ffn_ln(x, wg, wu, wd, g, b, l, alpha, tm):
    t, d = x.shape
    row = pl.BlockSpec((tm, d), lambda i: (i, 0))
    return pl.pallas_call(
        functools.partial(_ffn_kernel, alpha=alpha, f_chunk=256),
        grid=(t // tm,),
        in_specs=[row, _layer_spec(wg, l), _layer_spec(wu, l), _layer_spec(wd, l),
                  _layer_spec(g, l), _layer_spec(b, l)],
        out_specs=row,
        out_shape=jax.ShapeDtypeStruct((t, d), F32),
        compiler_params=_cparams("parallel"),
        name="ffn_ln",
    )(x, wg, wu, wd, g, b)


def _s5_kernel(x_ref, w_ref, bq_ref, cq_ref, are_ref, aim_ref, d_ref, o_ref,
               st_ref, carry_ref, u_ref):
    rows = x_ref.shape[0]
    nq = bq_ref.shape[0]
    half = bq_ref.shape[2] // 2
    cw = bq_ref.shape[1]

    @pl.when(pl.program_id(0) == 0)
    def _():
        carry_ref[...] = jnp.zeros_like(carry_ref)

    u = _mm(x_ref[...].astype(BF16), w_ref[...])
    u_ref[...] = u
    ub = u.astype(BF16)
    for q in range(nq):
        st_ref[:, 2 * half * q:2 * half * (q + 1)] = _mm(ub[:, cw * q:cw * (q + 1)], bq_ref[q])

    def step(t, s):
        r0 = pl.multiple_of(t * SUBLANES, SUBLANES)
        parts = []
        for q in range(nq):
            re = s[:, 2 * half * q:2 * half * q + half]
            im = s[:, 2 * half * q + half:2 * half * (q + 1)]
            ar = are_ref[:, half * q:half * (q + 1)]
            ai = aim_ref[:, half * q:half * (q + 1)]
            bre = st_ref[pl.ds(r0, SUBLANES), 2 * half * q:2 * half * q + half]
            bim = st_ref[pl.ds(r0, SUBLANES), 2 * half * q + half:2 * half * (q + 1)]
            parts.append(ar * re - ai * im + bre)
            parts.append(ar * im + ai * re + bim)
        s_new = jnp.concatenate(parts, axis=1)
        st_ref[pl.ds(r0, SUBLANES), :] = s_new
        return s_new

    carry_ref[...] = lax.fori_loop(0, rows // SUBLANES, step, carry_ref[...])

    for q in range(nq):
        y = _mm(st_ref[:, 2 * half * q:2 * half * (q + 1)].astype(BF16), cq_ref[q])
        y = y + d_ref[:, cw * q:cw * (q + 1)] * u_ref[:, cw * q:cw * (q + 1)]
        o_ref[:, cw * q:cw * (q + 1)] = jax.nn.gelu(y)


def _s5_branch(x, w_s5, bq, cq, a_re, a_im, d, l, rows):
    t, dm = x.shape
    wdt = w_s5.shape[2]
    n_state = a_re.shape[2]
    return pl.pallas_call(
        _s5_kernel,
        grid=(t // rows,),
        in_specs=[pl.BlockSpec((rows, dm), lambda i: (i, 0)),
                  _layer_spec(w_s5, l), _layer_spec(bq, l), _layer_spec(cq, l),
                  _layer_spec(a_re, l), _layer_spec(a_im, l), _layer_spec(d, l)],
        out_specs=pl.BlockSpec((rows, wdt), lambda i: (i, 0)),
        out_shape=jax.ShapeDtypeStruct((t, wdt), F32),
        scratch_shapes=[pltpu.VMEM((rows, 2 * n_state), F32),
                        pltpu.VMEM((SUBLANES, 2 * n_state), F32),
                        pltpu.VMEM((rows, wdt), F32)],
        compiler_params=_cparams("arbitrary"),
        name="s5_branch",
    )(x, w_s5, bq, cq, a_re, a_im, d)


def _lru_kernel(x_ref, wlx_ref, wly_ref, cw_ref, cb_ref, wa_ref, ba_ref, wx_ref, bx_ref,
                lam_ref, o_ref, hist_ref, h_ref, a_ref, b_ref):
    rows = x_ref.shape[0]
    hist_rows = hist_ref.shape[0]

    @pl.when(pl.program_id(0) == 0)
    def _():
        hist_ref[...] = jnp.zeros_like(hist_ref)
        h_ref[...] = jnp.zeros_like(h_ref)

    xb = x_ref[...].astype(BF16)
    zx = _mm(xb, wlx_ref[...])
    zy = _mm(xb, wly_ref[...])

    ext = jnp.concatenate([hist_ref[...], zx], axis=0)
    hist_ref[...] = zx[rows - hist_rows:, :]
    xc = cb_ref[...] + cw_ref[LRU_CONV - 1:LRU_CONV, :] * zx
    for j in range(1, LRU_CONV):
        off = hist_rows - SUBLANES * j
        xc = xc + cw_ref[LRU_CONV - 1 - j:LRU_CONV - j, :] * ext[off:off + rows, :]

    xcb = xc.astype(BF16)
    gate_r = jax.nn.sigmoid(_mm(xcb, wa_ref[...]) + ba_ref[...])
    gate_i = jax.nn.sigmoid(_mm(xcb, wx_ref[...]) + bx_ref[...])
    log_a = -LRU_C * gate_r * _softplus(-lam_ref[...])
    a = jnp.exp(log_a)
    a_ref[...] = a
    b_ref[...] = jnp.sqrt(-jnp.tanh(log_a) * (1.0 + a * a)) * gate_i * xc

    def step(t, h):
        r0 = pl.multiple_of(t * SUBLANES, SUBLANES)
        h = a_ref[pl.ds(r0, SUBLANES), :] * h + b_ref[pl.ds(r0, SUBLANES), :]
        b_ref[pl.ds(r0, SUBLANES), :] = h
        return h

    h_ref[...] = lax.fori_loop(0, rows // SUBLANES, step, h_ref[...], unroll=8)
    o_ref[...] = b_ref[...] * jax.nn.gelu(zy)


def _lru_branch(x, w_lx, w_ly, conv_w, conv_b, wa_bd, ba, wx_bd, bx, lam, l, rows):
    t, dm = x.shape
    wdt = w_lx.shape[2]
    params = (w_lx, w_ly, conv_w, conv_b, wa_bd, ba, wx_bd, bx, lam)
    return pl.pallas_call(
        _lru_kernel,
        grid=(t // rows,),
        in_specs=[pl.BlockSpec((rows, dm), lambda i: (i, 0))] + [_layer_spec(p, l) for p in params],
        out_specs=pl.BlockSpec((rows, wdt), lambda i: (i, 0)),
        out_shape=jax.ShapeDtypeStruct((t, wdt), F32),
        scratch_shapes=[pltpu.VMEM(((LRU_CONV - 1) * SUBLANES, wdt), F32),
                        pltpu.VMEM((SUBLANES, wdt), F32),
                        pltpu.VMEM((rows, wdt), F32),
                        pltpu.VMEM((rows, wdt), F32)],
        compiler_params=_cparams("arbitrary"),
        name="lru_branch",
    )(x, *params)


def _merge_batch_lanes(src_ref, blk_idx, i, lane_grp):
    bsz = src_ref.shape[0]
    grp = LANES // bsz
    tile = None
    for b in range(bsz):
        blk = src_ref[b, blk_idx]
        shift = ((b - i) % bsz) * grp
        if shift:
            blk = pltpu.roll(blk, shift, 1)
        tile = blk if tile is None else jnp.where(lane_grp == b, blk, tile)
    return tile


def _store_lane_blocks(dst_ref, b, blk0, val):
    for j in range(val.shape[1] // LANES):
        dst_ref[b, blk0 + j] = val[:, j * LANES:(j + 1) * LANES]


def _rw_prep_kernel(x_ref, w_ref, mu_ref, w0_ref, wup_ref, a0_ref, aup_ref, gup_ref,
                    kk_ref, ka_ref, rk_ref, ones_ref, sel_b_ref, sel_k_ref,
                    k5_ref, v2_ref, g_ref, bonus_ref, prev_ref, k5s_ref, vs_ref):
    tt = x_ref.shape[0]
    bsz = prev_ref.shape[0]
    cols = w_ref.shape[1]
    wdt = w0_ref.shape[1]
    grp = LANES // bsz
    lora0 = 5 * wdt
    lora1 = lora0 + wup_ref.shape[0]
    b = pl.program_id(1)

    @pl.when(pl.program_id(0) == 0)
    def _():
        prev_ref[b] = jnp.zeros(prev_ref.shape[1:], F32)

    z = _mm(x_ref[...].astype(BF16), w_ref[...])
    first_row = lax.broadcasted_iota(jnp.int32, (tt, cols), 0) == 0
    shifted = jnp.where(first_row, prev_ref[b, 0:1, :], pltpu.roll(z, 1, 0))
    prev_ref[b, 0:1, :] = z[tt - 1:tt, :]
    z = z + mu_ref[...] * (shifted - z)

    r = z[:, 0:wdt]
    k = z[:, wdt:2 * wdt]
    v_rep = z[:, 2 * wdt:4 * wdt]
    v = z[:, 4 * wdt:5 * wdt]
    lora = z[:, lora0:lora1]
    gd = z[:, lora1:]

    w_log = -_softplus(-(w0_ref[...] + _mm(jnp.tanh(lora).astype(BF16), wup_ref[...]))) - 0.5
    decay = jnp.exp(-jnp.exp(w_log))
    a = jax.nn.sigmoid(a0_ref[...] + _mm(lora.astype(BF16), aup_ref[...]))
    g_ref[...] = _mm(jax.nn.sigmoid(gd).astype(BF16), gup_ref[...])

    ones_h = ones_ref[...]
    kk = k * kk_ref[...]
    kk = kk * lax.rsqrt(_segsum(kk * kk, ones_h) + 1e-12)
    k = k * (1.0 + (a - 1.0) * ka_ref[...])
    kka = kk * a

    nb = wdt // LANES
    _store_lane_blocks(k5s_ref, b, 0, -kk)
    _store_lane_blocks(k5s_ref, b, nb, decay)
    _store_lane_blocks(k5s_ref, b, 2 * nb, kka)
    _store_lane_blocks(k5s_ref, b, 3 * nb, k)
    _store_lane_blocks(k5s_ref, b, 4 * nb, decay * r)
    _store_lane_blocks(vs_ref, b, 0, v_rep)
    vs_ref[b, 2 * nb] = _segsum(kka * r, sel_b_ref[...]) + _segsum(k * r, sel_k_ref[...])
    bonus_ref[...] = _segsum(r * k * rk_ref[...], ones_h) * v

    for blk in range(k5s_ref.shape[1]):
        @pl.when(b == bsz - 1)
        def _(blk=blk):
            lane_grp = lax.broadcasted_iota(jnp.int32, (tt, LANES), 1) // grp
            for i in range(bsz):
                k5_ref[blk * bsz + i] = _merge_batch_lanes(k5s_ref, blk, i, lane_grp)
    for blk in range(vs_ref.shape[1]):
        @pl.when(b == bsz - 1)
        def _(blk=blk):
            lane_grp = lax.broadcasted_iota(jnp.int32, (tt, LANES), 1) // grp
            for i in range(min(bsz, v2_ref.shape[1] - blk * bsz)):
                v2_ref[:, blk * bsz + i, :] = _merge_batch_lanes(vs_ref, blk, i, lane_grp)


def _rw_prep(x, w_rw, mu, w0, wup_pad, a0, aup_pad, gup, k_k, k_a, r_k, ones_h, sel_b, sel_k, l, bsz, tt):
    t, dm = x.shape
    seq = t // bsz
    wdt = w0.shape[2]
    cols = w_rw.shape[2]
    grp = LANES // bsz
    n_k5 = 5 * wdt // grp
    n_v2 = HEAD_DIM + 2
    params = (w_rw, mu, w0, wup_pad, a0, aup_pad, gup, k_k, k_a, r_k)
    per_batch = lambda n: pl.BlockSpec((tt, n), lambda i, b: (i, b))
    return pl.pallas_call(
        _rw_prep_kernel,
        grid=(seq // tt, bsz),
        in_specs=[per_batch(dm)] + [_layer_spec(p, l) for p in params]
                 + [_const_spec(ones_h.shape), _const_spec(sel_b.shape), _const_spec(sel_k.shape)],
        out_specs=[pl.BlockSpec((n_k5, tt, LANES), lambda i, b: (0, i, 0)),
                   pl.BlockSpec((tt, n_v2, LANES), lambda i, b: (i, 0, 0)),
                   per_batch(wdt), per_batch(wdt)],
        out_shape=[jax.ShapeDtypeStruct((n_k5, seq, LANES), F32),
                   jax.ShapeDtypeStruct((seq, n_v2, LANES), F32),
                   jax.ShapeDtypeStruct((seq, bsz * wdt), F32),
                   jax.ShapeDtypeStruct((seq, bsz * wdt), F32)],
        scratch_shapes=[pltpu.VMEM((bsz, SUBLANES, cols), F32),
                        pltpu.VMEM((bsz, 5 * wdt // LANES, tt, LANES), F32),
                        pltpu.VMEM((bsz, 2 * wdt // LANES + 1, tt, LANES), F32)],
        compiler_params=_cparams("arbitrary", "arbitrary"),
        name="rw_prep",
    )(x.reshape(seq, bsz * dm), *params, ones_h, sel_b, sel_k)


def _rw_scan_kernel(k5_ref, v_ref, y_ref, st_ref, ysc_ref, *, bsz, heads):
    tt = v_ref.shape[0]
    kp_n = st_ref.shape[0]
    hd = st_ref.shape[1]
    wdt = hd * heads
    grp = LANES // bsz

    @pl.when(pl.program_id(0) == 0)
    def _():
        st_ref[...] = jnp.zeros_like(st_ref)

    low_half = (lax.broadcasted_iota(jnp.int32, (hd, LANES), 1) // heads) % 2 == 0

    def fold(p):
        return p + jnp.where(low_half, pltpu.roll(p, LANES - heads, 1), pltpu.roll(p, heads, 1))

    def row(j, kp, t):
        return k5_ref[j * kp_n + kp, pl.ds(t, 1), :]

    def step(t, carry):
        vv = v_ref[t, 0:hd, :]
        b_r = v_ref[t, hd:hd + 1, :]
        k_r = v_ref[t, hd + 1:hd + 2, :]
        sa = jnp.zeros((hd, LANES), F32)
        yp = jnp.zeros((hd, LANES), F32)
        for kp in range(kp_n):
            s_k = st_ref[kp]
            sa = sa + s_k * row(0, kp, t)
            yp = yp + s_k * row(4, kp, t)
        sa = fold(sa)
        yp = fold(yp)
        for kp in range(kp_n):
            st_ref[kp] = st_ref[kp] * row(1, kp, t) + sa * row(2, kp, t) + vv * row(3, kp, t)
        ysc_ref[t] = yp + sa * b_r + vv * k_r
        return carry

    lax.fori_loop(0, tt, step, 0)

    per_blk = LANES // heads
    lane_v = lax.broadcasted_iota(jnp.int32, (tt, LANES), 1) // heads
    for b in range(bsz):
        for vb in range(hd // per_blk):
            tile = None
            for j in range(per_blk):
                src = ysc_ref[:, vb * per_blk + j, :]
                shift = (j * heads - b * grp) % LANES
                if shift:
                    src = pltpu.roll(src, shift, 1)
                tile = src if tile is None else jnp.where(lane_v == j, src, tile)
            y_ref[:, b * wdt + vb * LANES:b * wdt + (vb + 1) * LANES] = tile


def _rw_scan(k5, v2, bsz, heads, tt):
    n_k5, seq, _ = k5.shape
    n_v2 = v2.shape[1]
    hd = HEAD_DIM
    wdt = hd * heads
    return pl.pallas_call(
        functools.partial(_rw_scan_kernel, bsz=bsz, heads=heads),
        grid=(seq // tt,),
        in_specs=[pl.BlockSpec((n_k5, tt, LANES), lambda i: (0, i, 0)),
                  pl.BlockSpec((tt, n_v2, LANES), lambda i: (i, 0, 0))],
        out_specs=pl.BlockSpec((tt, bsz * wdt), lambda i: (i, 0)),
        out_shape=jax.ShapeDtypeStruct((seq, bsz * wdt), F32),
        scratch_shapes=[pltpu.VMEM((n_k5 // 5, hd, LANES), F32),
                        pltpu.VMEM((tt, hd, LANES), F32)],
        compiler_params=_cparams("arbitrary"),
        name="rw_scan",
    )(k5, v2)


def _merge_kernel(x_ref, ys5_ref, ylru_ref, yrw_ref, bonus_ref, g_ref,
                  wgate_ref, glu1_ref, glu2_ref, lruo_ref, rwo_ref, mix_ref,
                  gng_ref, gnb_ref, ones_ref, lng_ref, lnb_ref, o_ref, *, alpha):
    x = x_ref[...]
    dm = x.shape[1]
    gates = jax.nn.sigmoid(_mm(x.astype(BF16), wgate_ref[...]))

    ys5 = ys5_ref[...].astype(BF16)
    y_s5 = _mm(ys5, glu1_ref[...]) * jax.nn.sigmoid(_mm(ys5, glu2_ref[...]))
    y_lru = _mm(ylru_ref[...].astype(BF16), lruo_ref[...])

    ones_bd = ones_ref[...]
    inv_n = 1.0 / HEAD_DIM
    y = yrw_ref[...]
    yc = y - _segsum(y, ones_bd) * inv_n
    var = _segsum(yc * yc, ones_bd) * inv_n
    y = yc * lax.rsqrt(var + RW_GN_EPS) * gng_ref[...] + gnb_ref[...]
    y = (y + bonus_ref[...]) * g_ref[...]
    y_rw = _mm(y.astype(BF16), rwo_ref[...])

    merged = (gates[:, 0:dm] * y_s5 + gates[:, dm:2 * dm] * y_lru + gates[:, 2 * dm:3 * dm] * y_rw)
    o_ref[...] = _layer_norm(alpha * x + _mm(merged.astype(BF16), mix_ref[...]),
                             lng_ref[...], lnb_ref[...])


def _merge(x, ys5, ylru, yrw, bonus, g, w_gate, glu1, glu2, lru_o, rw_o, mix, gn_g, gn_b, ones_bd,
           ln_g, ln_b, l, alpha, tm):
    t, dm = x.shape
    wdt = ys5.shape[1]
    row = lambda n: pl.BlockSpec((tm, n), lambda i: (i, 0))
    params = (w_gate, glu1, glu2, lru_o, rw_o, mix, gn_g, gn_b)
    return pl.pallas_call(
        functools.partial(_merge_kernel, alpha=alpha),
        grid=(t // tm,),
        in_specs=[row(dm)] + [row(wdt)] * 5 + [_layer_spec(p, l) for p in params]
                 + [_const_spec(ones_bd.shape), _layer_spec(ln_g, l), _layer_spec(ln_b, l)],
        out_specs=row(dm),
        out_shape=jax.ShapeDtypeStruct((t, dm), F32),
        compiler_params=_cparams("parallel"),
        name="merge_mix_ln",
    )(x, ys5, ylru, yrw, bonus, g, *params, ones_bd, ln_g, ln_b)


def _kv_kernel(mem_ref, w_ref, o_ref):
    o_ref[...] = _mm(mem_ref[...].astype(BF16), w_ref[...]).astype(BF16)


def _kv_proj(mem2, wkv, tn):
    rows, dm = mem2.shape
    depth, _, n = wkv.shape
    return pl.pallas_call(
        _kv_kernel,
        grid=(depth, n // tn),
        in_specs=[pl.BlockSpec((rows, dm), lambda l, j: (0, 0)),
                  pl.BlockSpec((None, dm, tn), lambda l, j: (l, 0, j))],
        out_specs=pl.BlockSpec((None, rows, tn), lambda l, j: (l, 0, j)),
        out_shape=jax.ShapeDtypeStruct((depth, rows, n), BF16),
        compiler_params=_cparams("parallel", "parallel"),
        name="xa_kv_proj",
    )(mem2, wkv)


def _xattn_kernel(x_ref, k_ref, v_ref, wq_ref, wo_ref, g_ref, b_ref, o_ref, *, alpha, heads):
    x = x_ref[...]
    dm = x.shape[1]
    hd = dm // heads
    q = _mm(x.astype(BF16), wq_ref[...]).astype(BF16)
    outs = []
    for h in range(heads):
        sl = slice(h * hd, (h + 1) * hd)
        s = lax.dot_general(q[:, sl], k_ref[:, sl], (((1,), (1,)), ((), ())),
                            preferred_element_type=F32) * (hd ** -0.5)
        e = jnp.exp(s - jnp.max(s, -1, keepdims=True))
        p = e / jnp.sum(e, -1, keepdims=True)
        outs.append(_mm(p.astype(BF16), v_ref[:, sl]))
    o = jnp.concatenate(outs, axis=1)
    o_ref[...] = _layer_norm(alpha * x + _mm(o.astype(BF16), wo_ref[...]), g_ref[...], b_ref[...])


def _xattn(x, kv, wq, wo, g, b, l, alpha, bsz, ts):
    t, dm = x.shape
    s = t // bsz
    m = kv.shape[1] // bsz
    xv = x.reshape(s, bsz * dm)
    seq = pl.BlockSpec((ts, dm), lambda bi, i: (i, bi))
    out = pl.pallas_call(
        functools.partial(_xattn_kernel, alpha=alpha, heads=XA_HEADS),
        grid=(bsz, s // ts),
        in_specs=[seq,
                  pl.BlockSpec((None, m, dm), lambda bi, i: (l, bi, 0)),
                  pl.BlockSpec((None, m, dm), lambda bi, i: (l, bi, 1)),
                  _layer_spec(wq, l), _layer_spec(wo, l), _layer_spec(g, l), _layer_spec(b, l)],
        out_specs=seq,
        out_shape=jax.ShapeDtypeStruct((s, bsz * dm), F32),
        compiler_params=_cparams("parallel", "parallel"),
        name="xattn_ln",
    )(xv, kv, kv, wq, wo, g, b)
    return out.reshape(t, dm)


def _block_diag(w):
    depth, h, n, _ = w.shape
    eye = jnp.eye(h, dtype=w.dtype)
    return (w[:, :, :, None, :] * eye[None, :, None, :, None]).reshape(depth, h * n, h * n)


def _s5_discretise(lam_re, lam_im, log_dt, b_re, b_im, c_re, c_im):
    depth, g, p = lam_re.shape
    c = b_re.shape[-1]
    gb = LANES // c
    nq = g // gb
    dt = jnp.exp(log_dt)[..., None]
    mag = jnp.exp(lam_re * dt)
    a_re = mag * jnp.cos(lam_im * dt)
    a_im = mag * jnp.sin(lam_im * dt)
    den = lam_re * lam_re + lam_im * lam_im
    co_re = ((a_re - 1.0) * lam_re + a_im * lam_im) / den
    co_im = (a_im * lam_re - (a_re - 1.0) * lam_im) / den
    bb_re = co_re[..., None] * b_re - co_im[..., None] * b_im
    bb_im = co_re[..., None] * b_im + co_im[..., None] * b_re
    eye = jnp.eye(gb, dtype=F32)

    def b_layout(w):
        w = jnp.swapaxes(w.reshape(depth, nq, gb, p, c), 3, 4)
        return (w[:, :, :, :, None, :] * eye[None, None, :, None, :, None]).reshape(depth, nq, gb * c, gb * p)

    def c_layout(w):
        w = jnp.swapaxes(w.reshape(depth, nq, gb, c, p), 3, 4)
        return (w[:, :, :, :, None, :] * eye[None, None, :, None, :, None]).reshape(depth, nq, gb * p, gb * c)

    bq = jnp.concatenate([b_layout(bb_re), b_layout(bb_im)], axis=-1).astype(BF16)
    cq = jnp.concatenate([c_layout(c_re), c_layout(-c_im)], axis=-2).astype(BF16)
    return bq, cq, a_re.reshape(depth, 1, g * p), a_im.reshape(depth, 1, g * p)


def _tile(n, target):
    return min(n, target)


def kernel(x, mem, ffn1_wg, ffn1_wu, ffn1_wd, ln1_g, ln1_b, w_in, s5_lam_re, s5_lam_im, s5_log_dt, s5_b_re, s5_b_im, s5_c_re, s5_c_im, s5_d, s5_glu_w1, s5_glu_w2, lru_conv_w, lru_conv_b, lru_wa, lru_ba, lru_wx, lru_bx, lru_lambda, lru_w_out, rw_mu, rw_w0, rw_w_up, rw_a0, rw_a_up, rw_g_up, rw_k_k, rw_k_a, rw_r_k, rw_ln_g, rw_ln_b, rw_w_out, mix_w_out, ln2_g, ln2_b, xa_wq, xa_wkv, xa_wo, ln3_g, ln3_b, ffn2_wg, ffn2_wu, ffn2_wd, ln4_g, ln4_b):
    bsz, seq, dm = x.shape
    depth = w_in.shape[0]
    wdt = s5_d.shape[1]
    heads = wdt // HEAD_DIM
    alpha = (2 * depth) ** 0.25
    assert bsz == SUBLANES, "one time step must be one aligned 8-row group"
    t = bsz * seq

    bf = lambda w: w.astype(BF16)
    vec = lambda p: p.reshape(depth, 1, -1)

    w_in_b = bf(w_in)
    w_s5, w_lx, w_ly = w_in_b[:, :, 0:wdt], w_in_b[:, :, wdt:2 * wdt], w_in_b[:, :, 2 * wdt:3 * wdt]
    n_rw = rw_mu.shape[1]
    w_rw = w_in_b[:, :, 3 * wdt:3 * wdt + n_rw]
    w_gate = w_in_b[:, :, 3 * wdt + n_rw:]
    ffn1 = (bf(ffn1_wg), bf(ffn1_wu), bf(ffn1_wd), vec(ln1_g), vec(ln1_b))
    ffn2 = (bf(ffn2_wg), bf(ffn2_wu), bf(ffn2_wd), vec(ln4_g), vec(ln4_b))

    bq, cq, a_re, a_im = _s5_discretise(s5_lam_re, s5_lam_im, s5_log_dt, s5_b_re, s5_b_im, s5_c_re, s5_c_im)
    wa_bd, wx_bd = bf(_block_diag(lru_wa)), bf(_block_diag(lru_wx))

    kh_n = LANES // (bsz * heads)
    assert kh_n == 2 and HEAD_DIM % kh_n == 0, "lane layout b*16 + kh*8 + h needs batch * heads * 2 == 128"
    kp_n = HEAD_DIM // kh_n
    key_perm = np.array([h_ * HEAD_DIM + kh_ * kp_n + kp_
                         for kp_ in range(kp_n) for kh_ in range(kh_n) for h_ in range(heads)])
    val_perm = np.array([h_ * HEAD_DIM + v_ for v_ in range(HEAD_DIM) for h_ in range(heads)])
    val_rep_perm = np.array([h_ * HEAD_DIM + v_
                             for v_ in range(HEAD_DIM) for _ in range(kh_n) for h_ in range(heads)])

    def rw_layout(p):
        return jnp.concatenate([p[..., 0:wdt][..., key_perm], p[..., wdt:2 * wdt][..., key_perm],
                                p[..., 2 * wdt:3 * wdt][..., val_rep_perm], p[..., 2 * wdt:3 * wdt][..., val_perm],
                                p[..., 3 * wdt:]], axis=-1)

    w_rw_p = rw_layout(w_rw)
    mu_p = vec(rw_layout(rw_mu))
    zeros_dr = jnp.zeros((depth, RW_A_RANK, wdt), BF16)
    zeros_ar = jnp.zeros((depth, RW_DECAY_RANK, wdt), BF16)
    wup_pad = jnp.concatenate([bf(rw_w_up)[:, :, key_perm], zeros_dr], axis=1)
    aup_pad = jnp.concatenate([zeros_ar, bf(rw_a_up)[:, :, key_perm]], axis=1)
    gup_p = bf(rw_g_up)[:, :, val_perm]
    w0_p, a0_p = vec(rw_w0[:, key_perm]), vec(rw_a0[:, key_perm])
    kk_p, ka_p = vec(rw_k_k[:, key_perm]), vec(rw_k_a[:, key_perm])
    rk_p = vec(rw_r_k.reshape(depth, wdt)[:, key_perm])
    gn_g_p, gn_b_p = vec(rw_ln_g[:, val_perm]), vec(rw_ln_b[:, val_perm])
    rw_o = bf(rw_w_out)[:, val_perm, :]
    head_of = jnp.arange(wdt) % heads
    ones_h = (head_of[:, None] == head_of[None, :]).astype(BF16)
    lane = jnp.arange(LANES)
    grp = kh_n * heads
    sel_b = ((lane[None, :] < grp) & (lane[None, :] % heads == head_of[:, None])).astype(BF16)
    sel_k = ((lane[None, :] >= grp) & (lane[None, :] < 2 * grp)
             & (lane[None, :] % heads == head_of[:, None])).astype(BF16)

    kv = _kv_proj(mem.reshape(-1, dm), bf(xa_wkv), 512)
    wq_b, wo_b = bf(xa_wq), bf(xa_wo)
    glu1, glu2, lru_o, mix = bf(s5_glu_w1), bf(s5_glu_w2), bf(lru_w_out), bf(mix_w_out)

    tm = _tile(t, 512)

    h = jnp.transpose(x, (1, 0, 2)).reshape(t, dm)
    for l in range(depth):
        h = _ffn_ln(h, *ffn1, l, alpha, tm)

        y_s5 = _s5_branch(h, w_s5, bq, cq, a_re, a_im, vec(s5_d), l, _tile(t, 512))
        y_lru = _lru_branch(h, w_lx, w_ly, lru_conv_w, vec(lru_conv_b), wa_bd, vec(lru_ba), wx_bd,
                            vec(lru_bx), vec(lru_lambda), l, _tile(t, 1024))
        k5, v2, g, bonus = _rw_prep(h, w_rw_p, mu_p, w0_p, wup_pad, a0_p, aup_pad, gup_p, kk_p, ka_p, rk_p,
                                    ones_h, sel_b, sel_k, l, bsz, _tile(seq, 64))
        y_rw = _rw_scan(k5, v2, bsz, heads, _tile(seq, 64)).reshape(t, wdt)
        g, bonus = g.reshape(t, wdt), bonus.reshape(t, wdt)

        h = _merge(h, y_s5, y_lru, y_rw, bonus, g, w_gate, glu1, glu2, lru_o, rw_o, mix,
                   gn_g_p, gn_b_p, ones_h, vec(ln2_g), vec(ln2_b), l, alpha, tm)
        h = _xattn(h, kv, wq_b, wo_b, vec(ln3_g), vec(ln3_b), l, alpha, bsz, _tile(seq, 512))
        h = _ffn_ln(h, *ffn2, l, alpha, tm)
    return jnp.transpose(h.reshape(seq, bsz, dm), (1, 0, 2))
```

```python
import functools
import math

import jax
import jax.numpy as jnp
import numpy as np
from jax import lax
from jax.experimental import pallas as pl
from jax.experimental.pallas import tpu as pltpu

F32 = jnp.float32
BF16 = jnp.bfloat16

LN_EPS = 1e-5
RW_GN_EPS = 64e-5
LRU_C = 8.0
LRU_CONV = 4
S5_GROUP = 16
S5_STATE = 64
HEAD_DIM = 64
XA_HEADS = 4
RW_DECAY_RANK = 64
RW_A_RANK = 64
RW_GATE_RANK = 128

SUBLANES = 8
LANES = 128
VMEM_LIMIT_BYTES = 56 * 1024 * 1024
RELAYOUT_ROWS = 32


def _cparams(*sem):
    return pltpu.CompilerParams(dimension_semantics=sem, vmem_limit_bytes=VMEM_LIMIT_BYTES)


def _const_spec(shape):
    nd = len(shape)
    return pl.BlockSpec(shape, lambda *_: (0,) * nd, pipeline_mode=pl.Buffered(1))


def _layer_spec(arr, l):
    tail = arr.shape[1:]
    nd = len(tail)
    return pl.BlockSpec((None,) + tail, lambda *_: (l,) + (0,) * nd, pipeline_mode=pl.Buffered(1))


def _layer_norm(y, g, b):
    mu = jnp.mean(y, -1, keepdims=True)
    yc = y - mu
    var = jnp.mean(yc * yc, -1, keepdims=True)
    return yc * lax.rsqrt(var + LN_EPS) * g + b


def _mm(a, b):
    return jnp.dot(a, b, preferred_element_type=F32)


def _segsum(x, ones_bd):
    hi = x.astype(BF16)
    lo = (x - hi.astype(F32)).astype(BF16)
    return _mm(hi, ones_bd) + _mm(lo, ones_bd)


def _softplus(x):
    return jnp.maximum(x, 0.0) + jnp.log1p(jnp.exp(-jnp.abs(x)))


def _ffn_kernel(x_ref, wg_ref, wu_ref, wd_ref, g_ref, b_ref, o_ref, *, alpha, f_chunk):
    x = x_ref[...]
    xb = x.astype(BF16)
    d_ff = wg_ref.shape[1]
    acc = jnp.zeros(x.shape, F32)
    for c0 in range(0, d_ff, f_chunk):
        hg = _mm(xb, wg_ref[:, c0:c0 + f_chunk])
        hu = _mm(xb, wu_ref[:, c0:c0 + f_chunk])
        h = hg * jax.nn.sigmoid(hg) * hu
        acc = acc + _mm(h.astype(BF16), wd_ref[c0:c0 + f_chunk, :])
    o_ref[...] = _layer_norm(alpha * x + 0.5 * acc, g_ref[...], b_ref[...])


def _ffn_ln(x, wg, wu, wd, g, b, l, alpha, tm):
    t, d = x.shape
    row = pl.BlockSpec((tm, d), lambda i: (i, 0))
    return pl.pallas_call(
        functools.partial(_ffn_kernel, alpha=alpha, f_chunk=256),
        grid=(t // tm,),
        in_specs=[row, _layer_spec(wg, l), _layer_spec(wu, l), _layer_spec(wd, l),
                  _layer_spec(g, l), _layer_spec(b, l)],
        out_specs=row,
        out_shape=jax.ShapeDtypeStruct((t, d), F32),
        compiler_params=_cparams("parallel"),
        name="ffn_ln",
    )(x, wg, wu, wd, g, b)


def _s5_kernel(x_ref, w_ref, bq_ref, cq_ref, are_ref, aim_ref, d_ref, o_ref,
               st_ref, carry_ref, u_ref):
    rows = x_ref.shape[0]
    nq = bq_ref.shape[0]
    half = bq_ref.shape[2] // 2
    cw = bq_ref.shape[1]

    @pl.when(pl.program_id(0) == 0)
    def _():
        carry_ref[...] = jnp.zeros_like(carry_ref)

    u = _mm(x_ref[...].astype(BF16), w_ref[...])
    u_ref[...] = u
    ub = u.astype(BF16)
    for q in range(nq):
        st_ref[:, 2 * half * q:2 * half * (q + 1)] = _mm(ub[:, cw * q:cw * (q + 1)], bq_ref[q])

    def step(t, s):
        r0 = pl.multiple_of(t * SUBLANES, SUBLANES)
        parts = []
        for q in range(nq):
            re = s[:, 2 * half * q:2 * half * q + half]
            im = s[:, 2 * half * q + half:2 * half * (q + 1)]
            ar = are_ref[:, half * q:half * (q + 1)]
            ai = aim_ref[:, half * q:half * (q + 1)]
            bre = st_ref[pl.ds(r0, SUBLANES), 2 * half * q:2 * half * q + half]
            bim = st_ref[pl.ds(r0, SUBLANES), 2 * half * q + half:2 * half * (q + 1)]
            parts.append(ar * re - ai * im + bre)
            parts.append(ar * im + ai * re + bim)
        s_new = jnp.concatenate(parts, axis=1)
        st_ref[pl.ds(r0, SUBLANES), :] = s_new
        return s_new

    carry_ref[...] = lax.fori_loop(0, rows // SUBLANES, step, carry_ref[...])

    for q in range(nq):
        y = _mm(st_ref[:, 2 * half * q:2 * half * (q + 1)].astype(BF16), cq_ref[q])
        y = y + d_ref[:, cw * q:cw * (q + 1)] * u_ref[:, cw * q:cw * (q + 1)]
        o_ref[:, cw * q:cw * (q + 1)] = jax.nn.gelu(y)


def _s5_branch(x, w_s5, bq, cq, a_re, a_im, d, l, rows):
    t, dm = x.shape
    wdt = w_s5.shape[2]
    n_state = a_re.shape[2]
    return pl.pallas_call(
        _s5_kernel,
        grid=(t // rows,),
        in_specs=[pl.BlockSpec((rows, dm), lambda i: (i, 0)),
                  _layer_spec(w_s5, l), _layer_spec(bq, l), _layer_spec(cq, l),
                  _layer_spec(a_re, l), _layer_spec(a_im, l), _layer_spec(d, l)],
        out_specs=pl.BlockSpec((rows, wdt), lambda i: (i, 0)),
        out_shape=jax.ShapeDtypeStruct((t, wdt), F32),
        scratch_shapes=[pltpu.VMEM((rows, 2 * n_state), F32),
                        pltpu.VMEM((SUBLANES, 2 * n_state), F32),
                        pltpu.VMEM((rows, wdt), F32)],
        compiler_params=_cparams("arbitrary"),
        name="s5_branch",
    )(x, w_s5, bq, cq, a_re, a_im, d)


def _lru_kernel(x_ref, wlx_ref, wly_ref, cw_ref, cb_ref, wa_ref, ba_ref, wx_ref, bx_ref,
                lam_ref, o_ref, hist_ref, h_ref, a_ref, b_ref):
    rows = x_ref.shape[0]
    hist_rows = hist_ref.shape[0]

    @pl.when(pl.program_id(0) == 0)
    def _():
        hist_ref[...] = jnp.zeros_like(hist_ref)
        h_ref[...] = jnp.zeros_like(h_ref)

    xb = x_ref[...].astype(BF16)
    zx = _mm(xb, wlx_ref[...])
    zy = _mm(xb, wly_ref[...])

    ext = jnp.concatenate([hist_ref[...], zx], axis=0)
    hist_ref[...] = zx[rows - hist_rows:, :]
    xc = cb_ref[...] + cw_ref[LRU_CONV - 1:LRU_CONV, :] * zx
    for j in range(1, LRU_CONV):
        off = hist_rows - SUBLANES * j
        xc = xc + cw_ref[LRU_CONV - 1 - j:LRU_CONV - j, :] * ext[off:off + rows, :]

    xcb = xc.astype(BF16)
    gate_r = jax.nn.sigmoid(_mm(xcb, wa_ref[...]) + ba_ref[...])
    gate_i = jax.nn.sigmoid(_mm(xcb, wx_ref[...]) + bx_ref[...])
    log_a = -LRU_C * gate_r * _softplus(-lam_ref[...])
    a = jnp.exp(log_a)
    a_ref[...] = a
    b_ref[...] = jnp.sqrt(-jnp.tanh(log_a) * (1.0 + a * a)) * gate_i * xc

    def step(t, h):
        r0 = pl.multiple_of(t * SUBLANES, SUBLANES)
        h = a_ref[pl.ds(r0, SUBLANES), :] * h + b_ref[pl.ds(r0, SUBLANES), :]
        b_ref[pl.ds(r0, SUBLANES), :] = h
        return h

    h_ref[...] = lax.fori_loop(0, rows // SUBLANES, step, h_ref[...], unroll=8)
    o_ref[...] = b_ref[...] * jax.nn.gelu(zy)


def _lru_branch(x, w_lx, w_ly, conv_w, conv_b, wa_bd, ba, wx_bd, bx, lam, l, rows):
    t, dm = x.shape
    wdt = w_lx.shape[2]
    params = (w_lx, w_ly, conv_w, conv_b, wa_bd, ba, wx_bd, bx, lam)
    return pl.pallas_call(
        _lru_kernel,
        grid=(t // rows,),
        in_specs=[pl.BlockSpec((rows, dm), lambda i: (i, 0))] + [_layer_spec(p, l) for p in params],
        out_specs=pl.BlockSpec((rows, wdt), lambda i: (i, 0)),
        out_shape=jax.ShapeDtypeStruct((t, wdt), F32),
        scratch_shapes=[pltpu.VMEM(((LRU_CONV - 1) * SUBLANES, wdt), F32),
                        pltpu.VMEM((SUBLANES, wdt), F32),
                        pltpu.VMEM((rows, wdt), F32),
                        pltpu.VMEM((rows, wdt), F32)],
        compiler_params=_cparams("arbitrary"),
        name="lru_branch",
    )(x, *params)


def _swap_tile_and_lane_group(tiles, grp):
    n = len(tiles)
    lane_g = lax.broadcasted_iota(jnp.int32, tiles[0].shape, 1) // grp
    k = 1
    while k < n:
        high = (lane_g & k) != 0
        nxt = list(tiles)
        for p in range(n):
            if p & k == 0:
                lo_t, hi_t = tiles[p], tiles[p + k]
                nxt[p] = jnp.where(high, pltpu.roll(hi_t, k * grp, 1), lo_t)
                nxt[p + k] = jnp.where(high, hi_t, pltpu.roll(lo_t, LANES - k * grp, 1))
        tiles = nxt
        k *= 2
    return tiles


def _rw_prep_kernel(x_ref, w_ref, mu_ref, w0_ref, wup_ref, a0_ref, aup_ref, gup_ref,
                    kk_ref, ka_ref, rk_ref, ones_ref, sel_b_ref, sel_k_ref,
                    k5_ref, vs_ref, g_ref, bonus_ref, prev_ref):
    tt = x_ref.shape[0]
    cols = w_ref.shape[1]
    wdt = w0_ref.shape[1]
    lora0 = 5 * wdt
    lora1 = lora0 + wup_ref.shape[0]
    b = pl.program_id(1)

    @pl.when(pl.program_id(0) == 0)
    def _():
        prev_ref[b] = jnp.zeros(prev_ref.shape[1:], F32)

    z = _mm(x_ref[...].astype(BF16), w_ref[...])
    first_row = lax.broadcasted_iota(jnp.int32, (tt, cols), 0) == 0
    shifted = jnp.where(first_row, prev_ref[b, 0:1, :], pltpu.roll(z, 1, 0))
    prev_ref[b, 0:1, :] = z[tt - 1:tt, :]
    z = z + mu_ref[...] * (shifted - z)

    r = z[:, 0:wdt]
    k = z[:, wdt:2 * wdt]
    v_rep = z[:, 2 * wdt:4 * wdt]
    v = z[:, 4 * wdt:5 * wdt]
    lora = z[:, lora0:lora1]
    gd = z[:, lora1:]

    w_log = -_softplus(-(w0_ref[...] + _mm(jnp.tanh(lora).astype(BF16), wup_ref[...]))) - 0.5
    decay = jnp.exp(-jnp.exp(w_log))
    a = jax.nn.sigmoid(a0_ref[...] + _mm(lora.astype(BF16), aup_ref[...]))
    g_ref[...] = _mm(jax.nn.sigmoid(gd).astype(BF16), gup_ref[...])

    ones_h = ones_ref[...]
    kk = k * kk_ref[...]
    kk = kk * lax.rsqrt(_segsum(kk * kk, ones_h) + 1e-12)
    k = k * (1.0 + (a - 1.0) * ka_ref[...])
    kka = kk * a

    k5_ref[:, 0:wdt] = -kk
    k5_ref[:, wdt:2 * wdt] = decay
    k5_ref[:, 2 * wdt:3 * wdt] = kka
    k5_ref[:, 3 * wdt:4 * wdt] = k
    k5_ref[:, 4 * wdt:5 * wdt] = decay * r
    vs_ref[:, 0:2 * wdt] = v_rep
    vs_ref[:, 2 * wdt:2 * wdt + LANES] = _segsum(kka * r, sel_b_ref[...]) + _segsum(k * r, sel_k_ref[...])
    bonus_ref[...] = _segsum(r * k * rk_ref[...], ones_h) * v


def _rw_prep(x, w_rw, mu, w0, wup_pad, a0, aup_pad, gup, k_k, k_a, r_k, ones_h, sel_b, sel_k, l, bsz, tt):
    t, dm = x.shape
    seq = t // bsz
    wdt = w0.shape[2]
    cols = w_rw.shape[2]
    params = (w_rw, mu, w0, wup_pad, a0, aup_pad, gup, k_k, k_a, r_k)
    per_batch = lambda n: pl.BlockSpec((tt, n), lambda i, b: (i, b))
    widths = (5 * wdt, 2 * wdt + LANES, wdt, wdt)
    return pl.pallas_call(
        _rw_prep_kernel,
        grid=(seq // tt, bsz),
        in_specs=[per_batch(dm)] + [_layer_spec(p, l) for p in params]
                 + [_const_spec(ones_h.shape), _const_spec(sel_b.shape), _const_spec(sel_k.shape)],
        out_specs=[per_batch(n) for n in widths],
        out_shape=[jax.ShapeDtypeStruct((seq, bsz * n), F32) for n in widths],
        scratch_shapes=[pltpu.VMEM((bsz, SUBLANES, cols), F32)],
        compiler_params=_cparams("arbitrary", "arbitrary"),
        name="rw_prep",
    )(x.reshape(seq, bsz * dm), *params, ones_h, sel_b, sel_k)


def _rw_scan_kernel(k5_ref, vs_ref, y_ref, st_ref, k5p_ref, v2x_ref, ypl_ref, *, bsz, heads):
    tt = k5_ref.shape[0]
    kp_n = st_ref.shape[0]
    hd = st_ref.shape[1]
    wdt = hd * heads
    grp = LANES // bsz
    k5w = k5_ref.shape[1] // bsz
    vsw = vs_ref.shape[1] // bsz
    vt = hd // SUBLANES

    @pl.when(pl.program_id(0) == 0)
    def _():
        st_ref[...] = jnp.zeros_like(st_ref)

    rc = min(tt, RELAYOUT_ROWS)

    def batch_tiles(src_ref, width, blk, r0):
        return [src_ref[r0:r0 + rc, pl.ds(pl.multiple_of(b * width + blk * LANES, LANES), LANES)]
                for b in range(bsz)]

    def k5_block(blk, carry):
        for r0 in range(0, tt, rc):
            for i, tile in enumerate(_swap_tile_and_lane_group(batch_tiles(k5_ref, k5w, blk, r0), grp)):
                k5p_ref[blk * bsz + i, r0:r0 + rc, :] = tile
        return carry

    def vs_block(blk, carry):
        for r0 in range(0, tt, rc):
            for i, tile in enumerate(_swap_tile_and_lane_group(batch_tiles(vs_ref, vsw, blk, r0), grp)):
                v2x_ref[blk, r0:r0 + rc, i, :] = tile
        return carry

    lax.fori_loop(0, k5w // LANES, k5_block, 0)
    lax.fori_loop(0, vsw // LANES, vs_block, 0)

    low_half = (lax.broadcasted_iota(jnp.int32, (hd, LANES), 1) // heads) % 2 == 0

    def fold(p):
        return p + jnp.where(low_half, pltpu.roll(p, LANES - heads, 1), pltpu.roll(p, heads, 1))

    def row(j, kp, t):
        return k5p_ref[j * kp_n + kp, pl.ds(t, 1), :]

    def step(t, carry):
        vv = jnp.concatenate([v2x_ref[m, t] for m in range(vt)], axis=0)
        b_r = v2x_ref[vt, t, 0:1, :]
        k_r = v2x_ref[vt, t, 1:2, :]
        sa = jnp.zeros((hd, LANES), F32)
        yp = jnp.zeros((hd, LANES), F32)
        for kp in range(kp_n):
            s_k = st_ref[kp]
            sa = sa + s_k * row(0, kp, t)
            yp = yp + s_k * row(4, kp, t)
        sa = fold(sa)
        yp = fold(yp)
        for kp in range(kp_n):
            st_ref[kp] = st_ref[kp] * row(1, kp, t) + sa * row(2, kp, t) + vv * row(3, kp, t)
        y = yp + sa * b_r + vv * k_r
        for v in range(hd):
            ypl_ref[v, pl.ds(t, 1), :] = y[v:v + 1, :]
        return carry

    lax.fori_loop(0, tt, step, 0)

    pair_lo = lax.broadcasted_iota(jnp.int32, (rc, LANES), 1) % grp < heads

    def out_block(vb, carry):
        for r0 in range(0, tt, rc):
            pairs = []
            for m in range(bsz):
                even = ypl_ref[vb * 2 * bsz + 2 * m, r0:r0 + rc, :]
                odd = ypl_ref[vb * 2 * bsz + 2 * m + 1, r0:r0 + rc, :]
                pairs.append(jnp.where(pair_lo, even, pltpu.roll(odd, heads, 1)))
            for b, tile in enumerate(_swap_tile_and_lane_group(pairs, grp)):
                y_ref[r0:r0 + rc, pl.ds(pl.multiple_of(b * wdt + vb * LANES, LANES), LANES)] = tile
        return carry

    lax.fori_loop(0, wdt // LANES, out_block, 0)


def _rw_scan(k5, vs, bsz, heads, tt):
    seq = k5.shape[0]
    hd = HEAD_DIM
    wdt = hd * heads
    n_k5 = k5.shape[1] // LANES
    n_vs = vs.shape[1] // bsz // LANES
    return pl.pallas_call(
        functools.partial(_rw_scan_kernel, bsz=bsz, heads=heads),
        grid=(seq // tt,),
        in_specs=[pl.BlockSpec((tt, k5.shape[1]), lambda i: (i, 0)),
                  pl.BlockSpec((tt, vs.shape[1]), lambda i: (i, 0))],
        out_specs=pl.BlockSpec((tt, bsz * wdt), lambda i: (i, 0)),
        out_shape=jax.ShapeDtypeStruct((seq, bsz * wdt), F32),
        scratch_shapes=[pltpu.VMEM((n_k5 // 5, hd, LANES), F32),
                        pltpu.VMEM((n_k5, tt, LANES), F32),
                        pltpu.VMEM((n_vs, tt, bsz, LANES), F32),
                        pltpu.VMEM((hd, tt, LANES), F32)],
        compiler_params=_cparams("arbitrary"),
        name="rw_scan",
    )(k5, vs)


def _merge_kernel(x_ref, ys5_ref, ylru_ref, yrw_ref, bonus_ref, g_ref,
                  wgate_ref, glu1_ref, glu2_ref, lruo_ref, rwo_ref, mix_ref,
                  gng_ref, gnb_ref, ones_ref, lng_ref, lnb_ref, o_ref, *, alpha):
    x = x_ref[...]
    dm = x.shape[1]
    gates = jax.nn.sigmoid(_mm(x.astype(BF16), wgate_ref[...]))

    ys5 = ys5_ref[...].astype(BF16)
    y_s5 = _mm(ys5, glu1_ref[...]) * jax.nn.sigmoid(_mm(ys5, glu2_ref[...]))
    y_lru = _mm(ylru_ref[...].astype(BF16), lruo_ref[...])

    ones_bd = ones_ref[...]
    inv_n = 1.0 / HEAD_DIM
    y = yrw_ref[...]
    yc = y - _segsum(y, ones_bd) * inv_n
    var = _segsum(yc * yc, ones_bd) * inv_n
    y = yc * lax.rsqrt(var + RW_GN_EPS) * gng_ref[...] + gnb_ref[...]
    y = (y + bonus_ref[...]) * g_ref[...]
    y_rw = _mm(y.astype(BF16), rwo_ref[...])

    merged = (gates[:, 0:dm] * y_s5 + gates[:, dm:2 * dm] * y_lru + gates[:, 2 * dm:3 * dm] * y_rw)
    o_ref[...] = _layer_norm(alpha * x + _mm(merged.astype(BF16), mix_ref[...]),
                             lng_ref[...], lnb_ref[...])


def _merge(x, ys5, ylru, yrw, bonus, g, w_gate, glu1, glu2, lru_o, rw_o, mix, gn_g, gn_b, ones_bd,
           ln_g, ln_b, l, alpha, tm):
    t, dm = x.shape
    wdt = ys5.shape[1]
    row = lambda n: pl.BlockSpec((tm, n), lambda i: (i, 0))
    params = (w_gate, glu1, glu2, lru_o, rw_o, mix, gn_g, gn_b)
    return pl.pallas_call(
        functools.partial(_merge_kernel, alpha=alpha),
        grid=(t // tm,),
        in_specs=[row(dm)] + [row(wdt)] * 5 + [_layer_spec(p, l) for p in params]
                 + [_const_spec(ones_bd.shape), _layer_spec(ln_g, l), _layer_spec(ln_b, l)],
        out_specs=row(dm),
        out_shape=jax.ShapeDtypeStruct((t, dm), F32),
        compiler_params=_cparams("parallel"),
        name="merge_mix_ln",
    )(x, ys5, ylru, yrw, bonus, g, *params, ones_bd, ln_g, ln_b)


def _kv_kernel(mem_ref, w_ref, o_ref):
    o_ref[...] = _mm(mem_ref[...].astype(BF16), w_ref[...]).astype(BF16)


def _kv_proj(mem2, wkv, tn):
    rows, dm = mem2.shape
    depth, _, n = wkv.shape
    return pl.pallas_call(
        _kv_kernel,
        grid=(depth, n // tn),
        in_specs=[pl.BlockSpec((rows, dm), lambda l, j: (0, 0)),
                  pl.BlockSpec((None, dm, tn), lambda l, j: (l, 0, j))],
        out_specs=pl.BlockSpec((None, rows, tn), lambda l, j: (l, 0, j)),
        out_shape=jax.ShapeDtypeStruct((depth, rows, n), BF16),
        compiler_params=_cparams("parallel", "parallel"),
        name="xa_kv_proj",
    )(mem2, wkv)


def _xattn_kernel(x_ref, k_ref, v_ref, wq_ref, wo_ref, g_ref, b_ref, o_ref, *, alpha, heads):
    x = x_ref[...]
    dm = x.shape[1]
    hd = dm // heads
    q = _mm(x.astype(BF16), wq_ref[...]).astype(BF16)
    outs = []
    for h in range(heads):
        sl = slice(h * hd, (h + 1) * hd)
        s = lax.dot_general(q[:, sl], k_ref[:, sl], (((1,), (1,)), ((), ())),
                            preferred_element_type=F32) * (hd ** -0.5)
        e = jnp.exp(s - jnp.max(s, -1, keepdims=True))
        p = e / jnp.sum(e, -1, keepdims=True)
        outs.append(_mm(p.astype(BF16), v_ref[:, sl]))
    o = jnp.concatenate(outs, axis=1)
    o_ref[...] = _layer_norm(alpha * x + _mm(o.astype(BF16), wo_ref[...]), g_ref[...], b_ref[...])


def _xattn(x, kv, wq, wo, g, b, l, alpha, bsz, ts):
    t, dm = x.shape
    s = t // bsz
    m = kv.shape[1] // bsz
    xv = x.reshape(s, bsz * dm)
    seq = pl.BlockSpec((ts, dm), lambda bi, i: (i, bi))
    out = pl.pallas_call(
        functools.partial(_xattn_kernel, alpha=alpha, heads=XA_HEADS),
        grid=(bsz, s // ts),
        in_specs=[seq,
                  pl.BlockSpec((None, m, dm), lambda bi, i: (l, bi, 0)),
                  pl.BlockSpec((None, m, dm), lambda bi, i: (l, bi, 1)),
                  _layer_spec(wq, l), _layer_spec(wo, l), _layer_spec(g, l), _layer_spec(b, l)],
        out_specs=seq,
        out_shape=jax.ShapeDtypeStruct((s, bsz * dm), F32),
        compiler_params=_cparams("parallel", "parallel"),
        name="xattn_ln",
    )(xv, kv, kv, wq, wo, g, b)
    return out.reshape(t, dm)


def _block_diag(w):
    depth, h, n, _ = w.shape
    eye = jnp.eye(h, dtype=w.dtype)
    return (w[:, :, :, None, :] * eye[None, :, None, :, None]).reshape(depth, h * n, h * n)


def _s5_discretise(lam_re, lam_im, log_dt, b_re, b_im, c_re, c_im):
    depth, g, p = lam_re.shape
    c = b_re.shape[-1]
    gb = LANES // c
    nq = g // gb
    dt = jnp.exp(log_dt)[..., None]
    mag = jnp.exp(lam_re * dt)
    a_re = mag * jnp.cos(lam_im * dt)
    a_im = mag * jnp.sin(lam_im * dt)
    den = lam_re * lam_re + lam_im * lam_im
    co_re = ((a_re - 1.0) * lam_re + a_im * lam_im) / den
    co_im = (a_im * lam_re - (a_re - 1.0) * lam_im) / den
    bb_re = co_re[..., None] * b_re - co_im[..., None] * b_im
    bb_im = co_re[..., None] * b_im + co_im[..., None] * b_re
    eye = jnp.eye(gb, dtype=F32)

    def b_layout(w):
        w = jnp.swapaxes(w.reshape(depth, nq, gb, p, c), 3, 4)
        return (w[:, :, :, :, None, :] * eye[None, None, :, None, :, None]).reshape(depth, nq, gb * c, gb * p)

    def c_layout(w):
        w = jnp.swapaxes(w.reshape(depth, nq, gb, c, p), 3, 4)
        return (w[:, :, :, :, None, :] * eye[None, None, :, None, :, None]).reshape(depth, nq, gb * p, gb * c)

    bq = jnp.concatenate([b_layout(bb_re), b_layout(bb_im)], axis=-1).astype(BF16)
    cq = jnp.concatenate([c_layout(c_re), c_layout(-c_im)], axis=-2).astype(BF16)
    return bq, cq, a_re.reshape(depth, 1, g * p), a_im.reshape(depth, 1, g * p)


def _tile(n, target):
    return min(n, target)


def kernel(x, mem, ffn1_wg, ffn1_wu, ffn1_wd, ln1_g, ln1_b, w_in, s5_lam_re, s5_lam_im, s5_log_dt, s5_b_re, s5_b_im, s5_c_re, s5_c_im, s5_d, s5_glu_w1, s5_glu_w2, lru_conv_w, lru_conv_b, lru_wa, lru_ba, lru_wx, lru_bx, lru_lambda, lru_w_out, rw_mu, rw_w0, rw_w_up, rw_a0, rw_a_up, rw_g_up, rw_k_k, rw_k_a, rw_r_k, rw_ln_g, rw_ln_b, rw_w_out, mix_w_out, ln2_g, ln2_b, xa_wq, xa_wkv, xa_wo, ln3_g, ln3_b, ffn2_wg, ffn2_wu, ffn2_wd, ln4_g, ln4_b):
    bsz, seq, dm = x.shape
    depth = w_in.shape[0]
    wdt = s5_d.shape[1]
    heads = wdt // HEAD_DIM
    alpha = (2 * depth) ** 0.25
    assert bsz == SUBLANES, "one time step must be one aligned 8-row group"
    t = bsz * seq

    bf = lambda w: w.astype(BF16)
    vec = lambda p: p.reshape(depth, 1, -1)

    w_in_b = bf(w_in)
    w_s5, w_lx, w_ly = w_in_b[:, :, 0:wdt], w_in_b[:, :, wdt:2 * wdt], w_in_b[:, :, 2 * wdt:3 * wdt]
    n_rw = rw_mu.shape[1]
    w_rw = w_in_b[:, :, 3 * wdt:3 * wdt + n_rw]
    w_gate = w_in_b[:, :, 3 * wdt + n_rw:]
    ffn1 = (bf(ffn1_wg), bf(ffn1_wu), bf(ffn1_wd), vec(ln1_g), vec(ln1_b))
    ffn2 = (bf(ffn2_wg), bf(ffn2_wu), bf(ffn2_wd), vec(ln4_g), vec(ln4_b))

    bq, cq, a_re, a_im = _s5_discretise(s5_lam_re, s5_lam_im, s5_log_dt, s5_b_re, s5_b_im, s5_c_re, s5_c_im)
    wa_bd, wx_bd = bf(_block_diag(lru_wa)), bf(_block_diag(lru_wx))

    kh_n = LANES // (bsz * heads)
    assert kh_n == 2 and HEAD_DIM % kh_n == 0, "lane layout b*16 + kh*8 + h needs batch * heads * 2 == 128"
    kp_n = HEAD_DIM // kh_n
    key_perm = np.array([h_ * HEAD_DIM + kh_ * kp_n + kp_
                         for kp_ in range(kp_n) for kh_ in range(kh_n) for h_ in range(heads)])
    val_perm = np.array([h_ * HEAD_DIM + v_ for v_ in range(HEAD_DIM) for h_ in range(heads)])
    val_rep_perm = np.array([h_ * HEAD_DIM + v_
                             for v_ in range(HEAD_DIM) for _ in range(kh_n) for h_ in range(heads)])

    def rw_layout(p):
        return jnp.concatenate([p[..., 0:wdt][..., key_perm], p[..., wdt:2 * wdt][..., key_perm],
                                p[..., 2 * wdt:3 * wdt][..., val_rep_perm], p[..., 2 * wdt:3 * wdt][..., val_perm],
                                p[..., 3 * wdt:]], axis=-1)

    w_rw_p = rw_layout(w_rw)
    mu_p = vec(rw_layout(rw_mu))
    zeros_dr = jnp.zeros((depth, RW_A_RANK, wdt), BF16)
    zeros_ar = jnp.zeros((depth, RW_DECAY_RANK, wdt), BF16)
    wup_pad = jnp.concatenate([bf(rw_w_up)[:, :, key_perm], zeros_dr], axis=1)
    aup_pad = jnp.concatenate([zeros_ar, bf(rw_a_up)[:, :, key_perm]], axis=1)
    gup_p = bf(rw_g_up)[:, :, val_perm]
    w0_p, a0_p = vec(rw_w0[:, key_perm]), vec(rw_a0[:, key_perm])
    kk_p, ka_p = vec(rw_k_k[:, key_perm]), vec(rw_k_a[:, key_perm])
    rk_p = vec(rw_r_k.reshape(depth, wdt)[:, key_perm])
    gn_g_p, gn_b_p = vec(rw_ln_g[:, val_perm]), vec(rw_ln_b[:, val_perm])
    rw_o = bf(rw_w_out)[:, val_perm, :]
    head_of = jnp.arange(wdt) % heads
    ones_h = (head_of[:, None] == head_of[None, :]).astype(BF16)
    lane = jnp.arange(LANES)
    grp = kh_n * heads
    sel_b = ((lane[None, :] < grp) & (lane[None, :] % heads == head_of[:, None])).astype(BF16)
    sel_k = ((lane[None, :] >= grp) & (lane[None, :] < 2 * grp)
             & (lane[None, :] % heads == head_of[:, None])).astype(BF16)

    kv = _kv_proj(mem.reshape(-1, dm), bf(xa_wkv), 512)
    wq_b, wo_b = bf(xa_wq), bf(xa_wo)
    glu1, glu2, lru_o, mix = bf(s5_glu_w1), bf(s5_glu_w2), bf(lru_w_out), bf(mix_w_out)

    tm = _tile(t, 512)

    h = jnp.transpose(x, (1, 0, 2)).reshape(t, dm)
    for l in range(depth):
        h = _ffn_ln(h, *ffn1, l, alpha, tm)

        y_s5 = _s5_branch(h, w_s5, bq, cq, a_re, a_im, vec(s5_d), l, _tile(t, 512))
        y_lru = _lru_branch(h, w_lx, w_ly, lru_conv_w, vec(lru_conv_b), wa_bd, vec(lru_ba), wx_bd,
                            vec(lru_bx), vec(lru_lambda), l, _tile(t, 1024))
        k5, vs, g, bonus = _rw_prep(h, w_rw_p, mu_p, w0_p, wup_pad, a0_p, aup_pad, gup_p, kk_p, ka_p, rk_p,
                                    ones_h, sel_b, sel_k, l, bsz, _tile(seq, 512))
        y_rw = _rw_scan(k5, vs, bsz, heads, _tile(seq, 64)).reshape(t, wdt)
        g, bonus = g.reshape(t, wdt), bonus.reshape(t, wdt)

        h = _merge(h, y_s5, y_lru, y_rw, bonus, g, w_gate, glu1, glu2, lru_o, rw_o, mix,
                   gn_g_p, gn_b_p, ones_h, vec(ln2_g), vec(ln2_b), l, alpha, tm)
        h = _xattn(h, kv, wq_b, wo_b, vec(ln3_g), vec(ln3_b), l, alpha, bsz, _tile(seq, 512))
        h = _ffn_ln(h, *ffn2, l, alpha, tm)
    return jnp.transpose(h.reshape(seq, bsz, dm), (1, 0, 2))
```

```python
import functools
import math

import jax
import jax.numpy as jnp
import numpy as np
from jax import lax
from jax.experimental import pallas as pl
from jax.experimental.pallas import tpu as pltpu

F32 = jnp.float32
BF16 = jnp.bfloat16

LN_EPS = 1e-5
RW_GN_EPS = 64e-5
LRU_C = 8.0
LRU_CONV = 4
S5_GROUP = 16
S5_STATE = 64
HEAD_DIM = 64
XA_HEADS = 4
RW_DECAY_RANK = 64
RW_A_RANK = 64
RW_GATE_RANK = 128

SUBLANES = 8
LANES = 128
VMEM_LIMIT_BYTES = 56 * 1024 * 1024
RELAYOUT_ROWS = 32


def _cparams(*sem):
    return pltpu.CompilerParams(dimension_semantics=sem, vmem_limit_bytes=VMEM_LIMIT_BYTES)


def _const_spec(shape):
    nd = len(shape)
    return pl.BlockSpec(shape, lambda *_: (0,) * nd, pipeline_mode=pl.Buffered(1))


def _layer_spec(arr, l):
    tail = arr.shape[1:]
    nd = len(tail)
    return pl.BlockSpec((None,) + tail, lambda *_: (l,) + (0,) * nd, pipeline_mode=pl.Buffered(1))


def _layer_norm(y, g, b):
    mu = jnp.mean(y, -1, keepdims=True)
    yc = y - mu
    var = jnp.mean(yc * yc, -1, keepdims=True)
    return yc * lax.rsqrt(var + LN_EPS) * g + b


def _mm(a, b):
    return jnp.dot(a, b, preferred_element_type=F32)


def _segsum(x, ones_bd):
    hi = x.astype(BF16)
    lo = (x - hi.astype(F32)).astype(BF16)
    return _mm(hi, ones_bd) + _mm(lo, ones_bd)


def _softplus(x):
    return jnp.maximum(x, 0.0) + jnp.log1p(jnp.exp(-jnp.abs(x)))


def _ffn_kernel(x_ref, wg_ref, wu_ref, wd_ref, g_ref, b_ref, o_ref, *, alpha, f_chunk):
    x = x_ref[...]
    xb = x.astype(BF16)
    d_ff = wg_ref.shape[1]
    acc = jnp.zeros(x.shape, F32)
    for c0 in range(0, d_ff, f_chunk):
        hg = _mm(xb, wg_ref[:, c0:c0 + f_chunk])
        hu = _mm(xb, wu_ref[:, c0:c0 + f_chunk])
        h = hg * jax.nn.sigmoid(hg) * hu
        acc = acc + _mm(h.astype(BF16), wd_ref[c0:c0 + f_chunk, :])
    o_ref[...] = _layer_norm(alpha * x + 0.5 * acc, g_ref[...], b_ref[...])


def _seq_spec(tm, n):
    return pl.BlockSpec((tm, n), lambda i, b: (i, b))


def _batch_major_spec(tm, n):
    return pl.BlockSpec((None, tm, n), lambda i, b: (b, i, 0))


def _ffn_ln(x, wg, wu, wd, g, b, l, alpha, tm, bsz, batch_major_in=False, batch_major_out=False):
    d = wg.shape[1]
    seq = x.shape[1] if batch_major_in else x.shape[0]
    out_shape = (bsz, seq, d) if batch_major_out else (seq, bsz * d)
    return pl.pallas_call(
        functools.partial(_ffn_kernel, alpha=alpha, f_chunk=256),
        grid=(seq // tm, bsz),
        in_specs=[_batch_major_spec(tm, d) if batch_major_in else _seq_spec(tm, d),
                  _layer_spec(wg, l), _layer_spec(wu, l), _layer_spec(wd, l),
                  _layer_spec(g, l), _layer_spec(b, l)],
        out_specs=_batch_major_spec(tm, d) if batch_major_out else _seq_spec(tm, d),
        out_shape=jax.ShapeDtypeStruct(out_shape, F32),
        compiler_params=_cparams("parallel", "parallel"),
        name="ffn_ln",
    )(x, wg, wu, wd, g, b)


def _s5_kernel(x_ref, w_ref, bq_ref, cq_ref, are_ref, aim_ref, d_ref, o_ref,
               st_ref, carry_ref, u_ref):
    rows = x_ref.shape[0]
    nq = bq_ref.shape[0]
    half = bq_ref.shape[2] // 2
    cw = bq_ref.shape[1]

    @pl.when(pl.program_id(0) == 0)
    def _():
        carry_ref[...] = jnp.zeros_like(carry_ref)

    u = _mm(x_ref[...].astype(BF16), w_ref[...])
    u_ref[...] = u
    ub = u.astype(BF16)
    for q in range(nq):
        st_ref[:, 2 * half * q:2 * half * (q + 1)] = _mm(ub[:, cw * q:cw * (q + 1)], bq_ref[q])

    def step(t, s):
        r0 = pl.multiple_of(t * SUBLANES, SUBLANES)
        parts = []
        for q in range(nq):
            re = s[:, 2 * half * q:2 * half * q + half]
            im = s[:, 2 * half * q + half:2 * half * (q + 1)]
            ar = are_ref[:, half * q:half * (q + 1)]
            ai = aim_ref[:, half * q:half * (q + 1)]
            bre = st_ref[pl.ds(r0, SUBLANES), 2 * half * q:2 * half * q + half]
            bim = st_ref[pl.ds(r0, SUBLANES), 2 * half * q + half:2 * half * (q + 1)]
            parts.append(ar * re - ai * im + bre)
            parts.append(ar * im + ai * re + bim)
        s_new = jnp.concatenate(parts, axis=1)
        st_ref[pl.ds(r0, SUBLANES), :] = s_new
        return s_new

    carry_ref[...] = lax.fori_loop(0, rows // SUBLANES, step, carry_ref[...])

    for q in range(nq):
        y = _mm(st_ref[:, 2 * half * q:2 * half * (q + 1)].astype(BF16), cq_ref[q])
        y = y + d_ref[:, cw * q:cw * (q + 1)] * u_ref[:, cw * q:cw * (q + 1)]
        o_ref[:, cw * q:cw * (q + 1)] = jax.nn.gelu(y)


def _s5_branch(x, w_s5, bq, cq, a_re, a_im, d, l, rows):
    t, dm = x.shape
    wdt = w_s5.shape[2]
    n_state = a_re.shape[2]
    return pl.pallas_call(
        _s5_kernel,
        grid=(t // rows,),
        in_specs=[pl.BlockSpec((rows, dm), lambda i: (i, 0)),
                  _layer_spec(w_s5, l), _layer_spec(bq, l), _layer_spec(cq, l),
                  _layer_spec(a_re, l), _layer_spec(a_im, l), _layer_spec(d, l)],
        out_specs=pl.BlockSpec((rows, wdt), lambda i: (i, 0)),
        out_shape=jax.ShapeDtypeStruct((t, wdt), F32),
        scratch_shapes=[pltpu.VMEM((rows, 2 * n_state), F32),
                        pltpu.VMEM((SUBLANES, 2 * n_state), F32),
                        pltpu.VMEM((rows, wdt), F32)],
        compiler_params=_cparams("arbitrary"),
        name="s5_branch",
    )(x, w_s5, bq, cq, a_re, a_im, d)


def _lru_kernel(x_ref, wlx_ref, wly_ref, cw_ref, cb_ref, wa_ref, ba_ref, wx_ref, bx_ref,
                lam_ref, o_ref, hist_ref, h_ref, a_ref, b_ref):
    rows = x_ref.shape[0]
    hist_rows = hist_ref.shape[0]

    @pl.when(pl.program_id(0) == 0)
    def _():
        hist_ref[...] = jnp.zeros_like(hist_ref)
        h_ref[...] = jnp.zeros_like(h_ref)

    xb = x_ref[...].astype(BF16)
    zx = _mm(xb, wlx_ref[...])
    zy = _mm(xb, wly_ref[...])

    ext = jnp.concatenate([hist_ref[...], zx], axis=0)
    hist_ref[...] = zx[rows - hist_rows:, :]
    xc = cb_ref[...] + cw_ref[LRU_CONV - 1:LRU_CONV, :] * zx
    for j in range(1, LRU_CONV):
        off = hist_rows - SUBLANES * j
        xc = xc + cw_ref[LRU_CONV - 1 - j:LRU_CONV - j, :] * ext[off:off + rows, :]

    xcb = xc.astype(BF16)
    gate_r = jax.nn.sigmoid(_mm(xcb, wa_ref[...]) + ba_ref[...])
    gate_i = jax.nn.sigmoid(_mm(xcb, wx_ref[...]) + bx_ref[...])
    log_a = -LRU_C * gate_r * _softplus(-lam_ref[...])
    a = jnp.exp(log_a)
    a_ref[...] = a
    b_ref[...] = jnp.sqrt(-jnp.tanh(log_a) * (1.0 + a * a)) * gate_i * xc

    def step(t, h):
        r0 = pl.multiple_of(t * SUBLANES, SUBLANES)
        h = a_ref[pl.ds(r0, SUBLANES), :] * h + b_ref[pl.ds(r0, SUBLANES), :]
        b_ref[pl.ds(r0, SUBLANES), :] = h
        return h

    h_ref[...] = lax.fori_loop(0, rows // SUBLANES, step, h_ref[...], unroll=8)
    o_ref[...] = b_ref[...] * jax.nn.gelu(zy)


def _lru_branch(x, w_lx, w_ly, conv_w, conv_b, wa_bd, ba, wx_bd, bx, lam, l, rows):
    t, dm = x.shape
    wdt = w_lx.shape[2]
    params = (w_lx, w_ly, conv_w, conv_b, wa_bd, ba, wx_bd, bx, lam)
    return pl.pallas_call(
        _lru_kernel,
        grid=(t // rows,),
        in_specs=[pl.BlockSpec((rows, dm), lambda i: (i, 0))] + [_layer_spec(p, l) for p in params],
        out_specs=pl.BlockSpec((rows, wdt), lambda i: (i, 0)),
        out_shape=jax.ShapeDtypeStruct((t, wdt), F32),
        scratch_shapes=[pltpu.VMEM(((LRU_CONV - 1) * SUBLANES, wdt), F32),
                        pltpu.VMEM((SUBLANES, wdt), F32),
                        pltpu.VMEM((rows, wdt), F32),
                        pltpu.VMEM((rows, wdt), F32)],
        compiler_params=_cparams("arbitrary"),
        name="lru_branch",
    )(x, *params)


def _swap_tile_and_lane_group(tiles, grp):
    n = len(tiles)
    lane_g = lax.broadcasted_iota(jnp.int32, tiles[0].shape, 1) // grp
    k = 1
    while k < n:
        high = (lane_g & k) != 0
        nxt = list(tiles)
        for p in range(n):
            if p & k == 0:
                lo_t, hi_t = tiles[p], tiles[p + k]
                nxt[p] = jnp.where(high, pltpu.roll(hi_t, k * grp, 1), lo_t)
                nxt[p + k] = jnp.where(high, hi_t, pltpu.roll(lo_t, LANES - k * grp, 1))
        tiles = nxt
        k *= 2
    return tiles


def _rw_prep_kernel(x_ref, w_ref, mu_ref, w0_ref, wup_ref, a0_ref, aup_ref, gup_ref,
                    kk_ref, ka_ref, rk_ref, ones_ref, sel_b_ref, sel_k_ref,
                    k5_ref, vs_ref, g_ref, bonus_ref, prev_ref):
    tt = x_ref.shape[0]
    cols = w_ref.shape[1]
    wdt = w0_ref.shape[1]
    lora0 = 5 * wdt
    lora1 = lora0 + wup_ref.shape[0]
    b = pl.program_id(1)

    @pl.when(pl.program_id(0) == 0)
    def _():
        prev_ref[b] = jnp.zeros(prev_ref.shape[1:], F32)

    z = _mm(x_ref[...].astype(BF16), w_ref[...])
    first_row = lax.broadcasted_iota(jnp.int32, (tt, cols), 0) == 0
    shifted = jnp.where(first_row, prev_ref[b, 0:1, :], pltpu.roll(z, 1, 0))
    prev_ref[b, 0:1, :] = z[tt - 1:tt, :]
    z = z + mu_ref[...] * (shifted - z)

    r = z[:, 0:wdt]
    k = z[:, wdt:2 * wdt]
    v_rep = z[:, 2 * wdt:4 * wdt]
    v = z[:, 4 * wdt:5 * wdt]
    lora = z[:, lora0:lora1]
    gd = z[:, lora1:]

    w_log = -_softplus(-(w0_ref[...] + _mm(jnp.tanh(lora).astype(BF16), wup_ref[...]))) - 0.5
    decay = jnp.exp(-jnp.exp(w_log))
    a = jax.nn.sigmoid(a0_ref[...] + _mm(lora.astype(BF16), aup_ref[...]))
    g_ref[...] = _mm(jax.nn.sigmoid(gd).astype(BF16), gup_ref[...])

    ones_h = ones_ref[...]
    kk = k * kk_ref[...]
    kk = kk * lax.rsqrt(_segsum(kk * kk, ones_h) + 1e-12)
    k = k * (1.0 + (a - 1.0) * ka_ref[...])
    kka = kk * a

    k5_ref[:, 0:wdt] = -kk
    k5_ref[:, wdt:2 * wdt] = decay
    k5_ref[:, 2 * wdt:3 * wdt] = kka
    k5_ref[:, 3 * wdt:4 * wdt] = k
    k5_ref[:, 4 * wdt:5 * wdt] = decay * r
    vs_ref[:, 0:2 * wdt] = v_rep
    vs_ref[:, 2 * wdt:2 * wdt + LANES] = _segsum(kka * r, sel_b_ref[...]) + _segsum(k * r, sel_k_ref[...])
    bonus_ref[...] = _segsum(r * k * rk_ref[...], ones_h) * v


def _rw_prep(x, w_rw, mu, w0, wup_pad, a0, aup_pad, gup, k_k, k_a, r_k, ones_h, sel_b, sel_k, l, bsz, tt):
    seq = x.shape[0]
    dm = x.shape[1] // bsz
    wdt = w0.shape[2]
    cols = w_rw.shape[2]
    params = (w_rw, mu, w0, wup_pad, a0, aup_pad, gup, k_k, k_a, r_k)
    per_batch = lambda n: _seq_spec(tt, n)
    widths = (5 * wdt, 2 * wdt + LANES, wdt, wdt)
    return pl.pallas_call(
        _rw_prep_kernel,
        grid=(seq // tt, bsz),
        in_specs=[per_batch(dm)] + [_layer_spec(p, l) for p in params]
                 + [_const_spec(ones_h.shape), _const_spec(sel_b.shape), _const_spec(sel_k.shape)],
        out_specs=[per_batch(n) for n in widths],
        out_shape=[jax.ShapeDtypeStruct((seq, bsz * n), F32) for n in widths],
        scratch_shapes=[pltpu.VMEM((bsz, SUBLANES, cols), F32)],
        compiler_params=_cparams("arbitrary", "arbitrary"),
        name="rw_prep",
    )(x, *params, ones_h, sel_b, sel_k)


def _rw_scan_kernel(k5_ref, vs_ref, y_ref, st_ref, k5p_ref, v2x_ref, ypl_ref, *, bsz, heads):
    tt = k5_ref.shape[0]
    kp_n = st_ref.shape[0]
    hd = st_ref.shape[1]
    wdt = hd * heads
    grp = LANES // bsz
    k5w = k5_ref.shape[1] // bsz
    vsw = vs_ref.shape[1] // bsz
    vt = hd // SUBLANES

    @pl.when(pl.program_id(0) == 0)
    def _():
        st_ref[...] = jnp.zeros_like(st_ref)

    rc = min(tt, RELAYOUT_ROWS)

    def batch_tiles(src_ref, width, blk, r0):
        return [src_ref[r0:r0 + rc, pl.ds(pl.multiple_of(b * width + blk * LANES, LANES), LANES)]
                for b in range(bsz)]

    def k5_block(blk, carry):
        for r0 in range(0, tt, rc):
            for i, tile in enumerate(_swap_tile_and_lane_group(batch_tiles(k5_ref, k5w, blk, r0), grp)):
                k5p_ref[blk * bsz + i, r0:r0 + rc, :] = tile
        return carry

    def vs_block(blk, carry):
        for r0 in range(0, tt, rc):
            for i, tile in enumerate(_swap_tile_and_lane_group(batch_tiles(vs_ref, vsw, blk, r0), grp)):
                v2x_ref[blk, r0:r0 + rc, i, :] = tile
        return carry

    lax.fori_loop(0, k5w // LANES, k5_block, 0)
    lax.fori_loop(0, vsw // LANES, vs_block, 0)

    low_half = (lax.broadcasted_iota(jnp.int32, (hd, LANES), 1) // heads) % 2 == 0

    def fold(p):
        return p + jnp.where(low_half, pltpu.roll(p, LANES - heads, 1), pltpu.roll(p, heads, 1))

    def row(j, kp, t):
        return k5p_ref[j * kp_n + kp, pl.ds(t, 1), :]

    def step(t, carry):
        vv = jnp.concatenate([v2x_ref[m, t] for m in range(vt)], axis=0)
        b_r = v2x_ref[vt, t, 0:1, :]
        k_r = v2x_ref[vt, t, 1:2, :]
        sa = jnp.zeros((hd, LANES), F32)
        yp = jnp.zeros((hd, LANES), F32)
        for kp in range(kp_n):
            s_k = st_ref[kp]
            sa = sa + s_k * row(0, kp, t)
            yp = yp + s_k * row(4, kp, t)
        sa = fold(sa)
        yp = fold(yp)
        for kp in range(kp_n):
            st_ref[kp] = st_ref[kp] * row(1, kp, t) + sa * row(2, kp, t) + vv * row(3, kp, t)
        y = yp + sa * b_r + vv * k_r
        for v in range(hd):
            ypl_ref[v, pl.ds(t, 1), :] = y[v:v + 1, :]
        return carry

    lax.fori_loop(0, tt, step, 0)

    pair_lo = lax.broadcasted_iota(jnp.int32, (rc, LANES), 1) % grp < heads

    def out_block(vb, carry):
        for r0 in range(0, tt, rc):
            pairs = []
            for m in range(bsz):
                even = ypl_ref[vb * 2 * bsz + 2 * m, r0:r0 + rc, :]
                odd = ypl_ref[vb * 2 * bsz + 2 * m + 1, r0:r0 + rc, :]
                pairs.append(jnp.where(pair_lo, even, pltpu.roll(odd, heads, 1)))
            for b, tile in enumerate(_swap_tile_and_lane_group(pairs, grp)):
                y_ref[r0:r0 + rc, pl.ds(pl.multiple_of(b * wdt + vb * LANES, LANES), LANES)] = tile
        return carry

    lax.fori_loop(0, wdt // LANES, out_block, 0)


def _rw_scan(k5, vs, bsz, heads, tt):
    seq = k5.shape[0]
    hd = HEAD_DIM
    wdt = hd * heads
    n_k5 = k5.shape[1] // LANES
    n_vs = vs.shape[1] // bsz // LANES
    return pl.pallas_call(
        functools.partial(_rw_scan_kernel, bsz=bsz, heads=heads),
        grid=(seq // tt,),
        in_specs=[pl.BlockSpec((tt, k5.shape[1]), lambda i: (i, 0)),
                  pl.BlockSpec((tt, vs.shape[1]), lambda i: (i, 0))],
        out_specs=pl.BlockSpec((tt, bsz * wdt), lambda i: (i, 0)),
        out_shape=jax.ShapeDtypeStruct((seq, bsz * wdt), F32),
        scratch_shapes=[pltpu.VMEM((n_k5 // 5, hd, LANES), F32),
                        pltpu.VMEM((n_k5, tt, LANES), F32),
                        pltpu.VMEM((n_vs, tt, bsz, LANES), F32),
                        pltpu.VMEM((hd, tt, LANES), F32)],
        compiler_params=_cparams("arbitrary"),
        name="rw_scan",
    )(k5, vs)


def _merge_kernel(x_ref, ys5_ref, ylru_ref, yrw_ref, bonus_ref, g_ref,
                  wgate_ref, glu1_ref, glu2_ref, lruo_ref, rwo_ref, mix_ref,
                  gng_ref, gnb_ref, ones_ref, lng_ref, lnb_ref, o_ref, *, alpha):
    x = x_ref[...]
    dm = x.shape[1]
    gates = jax.nn.sigmoid(_mm(x.astype(BF16), wgate_ref[...]))

    ys5 = ys5_ref[...].astype(BF16)
    y_s5 = _mm(ys5, glu1_ref[...]) * jax.nn.sigmoid(_mm(ys5, glu2_ref[...]))
    y_lru = _mm(ylru_ref[...].astype(BF16), lruo_ref[...])

    ones_bd = ones_ref[...]
    inv_n = 1.0 / HEAD_DIM
    y = yrw_ref[...]
    yc = y - _segsum(y, ones_bd) * inv_n
    var = _segsum(yc * yc, ones_bd) * inv_n
    y = yc * lax.rsqrt(var + RW_GN_EPS) * gng_ref[...] + gnb_ref[...]
    y = (y + bonus_ref[...]) * g_ref[...]
    y_rw = _mm(y.astype(BF16), rwo_ref[...])

    merged = (gates[:, 0:dm] * y_s5 + gates[:, dm:2 * dm] * y_lru + gates[:, 2 * dm:3 * dm] * y_rw)
    o_ref[...] = _layer_norm(alpha * x + _mm(merged.astype(BF16), mix_ref[...]),
                             lng_ref[...], lnb_ref[...])


def _merge(x, ys5, ylru, yrw, bonus, g, w_gate, glu1, glu2, lru_o, rw_o, mix, gn_g, gn_b, ones_bd,
           ln_g, ln_b, l, alpha, tm, bsz):
    seq = x.shape[0]
    dm = x.shape[1] // bsz
    wdt = ys5.shape[1] // bsz
    params = (w_gate, glu1, glu2, lru_o, rw_o, mix, gn_g, gn_b)
    return pl.pallas_call(
        functools.partial(_merge_kernel, alpha=alpha),
        grid=(seq // tm, bsz),
        in_specs=[_seq_spec(tm, dm)] + [_seq_spec(tm, wdt)] * 5 + [_layer_spec(p, l) for p in params]
                 + [_const_spec(ones_bd.shape), _layer_spec(ln_g, l), _layer_spec(ln_b, l)],
        out_specs=_seq_spec(tm, dm),
        out_shape=jax.ShapeDtypeStruct((seq, bsz * dm), F32),
        compiler_params=_cparams("parallel", "parallel"),
        name="merge_mix_ln",
    )(x, ys5, ylru, yrw, bonus, g, *params, ones_bd, ln_g, ln_b)


def _kv_kernel(mem_ref, w_ref, o_ref):
    o_ref[...] = _mm(mem_ref[...].astype(BF16), w_ref[...]).astype(BF16)


def _kv_proj(mem2, wkv, tn):
    rows, dm = mem2.shape
    depth, _, n = wkv.shape
    return pl.pallas_call(
        _kv_kernel,
        grid=(depth, n // tn),
        in_specs=[pl.BlockSpec((rows, dm), lambda l, j: (0, 0)),
                  pl.BlockSpec((None, dm, tn), lambda l, j: (l, 0, j))],
        out_specs=pl.BlockSpec((None, rows, tn), lambda l, j: (l, 0, j)),
        out_shape=jax.ShapeDtypeStruct((depth, rows, n), BF16),
        compiler_params=_cparams("parallel", "parallel"),
        name="xa_kv_proj",
    )(mem2, wkv)


def _xattn_kernel(x_ref, k_ref, v_ref, wq_ref, wo_ref, g_ref, b_ref, o_ref, *, alpha, heads):
    x = x_ref[...]
    dm = x.shape[1]
    hd = dm // heads
    q = _mm(x.astype(BF16), wq_ref[...]).astype(BF16)
    outs = []
    for h in range(heads):
        sl = slice(h * hd, (h + 1) * hd)
        s = lax.dot_general(q[:, sl], k_ref[:, sl], (((1,), (1,)), ((), ())),
                            preferred_element_type=F32) * (hd ** -0.5)
        e = jnp.exp(s - jnp.max(s, -1, keepdims=True))
        p = e / jnp.sum(e, -1, keepdims=True)
        outs.append(_mm(p.astype(BF16), v_ref[:, sl]))
    o = jnp.concatenate(outs, axis=1)
    o_ref[...] = _layer_norm(alpha * x + _mm(o.astype(BF16), wo_ref[...]), g_ref[...], b_ref[...])


def _xattn(x, kv, wq, wo, g, b, l, alpha, bsz, ts):
    s = x.shape[0]
    dm = x.shape[1] // bsz
    m = kv.shape[1] // bsz
    seq = pl.BlockSpec((ts, dm), lambda bi, i: (i, bi))
    return pl.pallas_call(
        functools.partial(_xattn_kernel, alpha=alpha, heads=XA_HEADS),
        grid=(bsz, s // ts),
        in_specs=[seq,
                  pl.BlockSpec((None, m, dm), lambda bi, i: (l, bi, 0)),
                  pl.BlockSpec((None, m, dm), lambda bi, i: (l, bi, 1)),
                  _layer_spec(wq, l), _layer_spec(wo, l), _layer_spec(g, l), _layer_spec(b, l)],
        out_specs=seq,
        out_shape=jax.ShapeDtypeStruct((s, bsz * dm), F32),
        compiler_params=_cparams("parallel", "parallel"),
        name="xattn_ln",
    )(x, kv, kv, wq, wo, g, b)


def _block_diag(w):
    depth, h, n, _ = w.shape
    eye = jnp.eye(h, dtype=w.dtype)
    return (w[:, :, :, None, :] * eye[None, :, None, :, None]).reshape(depth, h * n, h * n)


def _s5_discretise(lam_re, lam_im, log_dt, b_re, b_im, c_re, c_im):
    depth, g, p = lam_re.shape
    c = b_re.shape[-1]
    gb = LANES // c
    nq = g // gb
    dt = jnp.exp(log_dt)[..., None]
    mag = jnp.exp(lam_re * dt)
    a_re = mag * jnp.cos(lam_im * dt)
    a_im = mag * jnp.sin(lam_im * dt)
    den = lam_re * lam_re + lam_im * lam_im
    co_re = ((a_re - 1.0) * lam_re + a_im * lam_im) / den
    co_im = (a_im * lam_re - (a_re - 1.0) * lam_im) / den
    bb_re = co_re[..., None] * b_re - co_im[..., None] * b_im
    bb_im = co_re[..., None] * b_im + co_im[..., None] * b_re
    eye = jnp.eye(gb, dtype=F32)

    def b_layout(w):
        w = jnp.swapaxes(w.reshape(depth, nq, gb, p, c), 3, 4)
        return (w[:, :, :, :, None, :] * eye[None, None, :, None, :, None]).reshape(depth, nq, gb * c, gb * p)

    def c_layout(w):
        w = jnp.swapaxes(w.reshape(depth, nq, gb, c, p), 3, 4)
        return (w[:, :, :, :, None, :] * eye[None, None, :, None, :, None]).reshape(depth, nq, gb * p, gb * c)

    bq = jnp.concatenate([b_layout(bb_re), b_layout(bb_im)], axis=-1).astype(BF16)
    cq = jnp.concatenate([c_layout(c_re), c_layout(-c_im)], axis=-2).astype(BF16)
    return bq, cq, a_re.reshape(depth, 1, g * p), a_im.reshape(depth, 1, g * p)


def _tile(n, target):
    return min(n, target)


def kernel(x, mem, ffn1_wg, ffn1_wu, ffn1_wd, ln1_g, ln1_b, w_in, s5_lam_re, s5_lam_im, s5_log_dt, s5_b_re, s5_b_im, s5_c_re, s5_c_im, s5_d, s5_glu_w1, s5_glu_w2, lru_conv_w, lru_conv_b, lru_wa, lru_ba, lru_wx, lru_bx, lru_lambda, lru_w_out, rw_mu, rw_w0, rw_w_up, rw_a0, rw_a_up, rw_g_up, rw_k_k, rw_k_a, rw_r_k, rw_ln_g, rw_ln_b, rw_w_out, mix_w_out, ln2_g, ln2_b, xa_wq, xa_wkv, xa_wo, ln3_g, ln3_b, ffn2_wg, ffn2_wu, ffn2_wd, ln4_g, ln4_b):
    bsz, seq, dm = x.shape
    depth = w_in.shape[0]
    wdt = s5_d.shape[1]
    heads = wdt // HEAD_DIM
    alpha = (2 * depth) ** 0.25
    assert bsz == SUBLANES, "one time step must be one aligned 8-row group"
    t = bsz * seq

    bf = lambda w: w.astype(BF16)
    vec = lambda p: p.reshape(depth, 1, -1)

    w_in_b = bf(w_in)
    w_s5, w_lx, w_ly = w_in_b[:, :, 0:wdt], w_in_b[:, :, wdt:2 * wdt], w_in_b[:, :, 2 * wdt:3 * wdt]
    n_rw = rw_mu.shape[1]
    w_rw = w_in_b[:, :, 3 * wdt:3 * wdt + n_rw]
    w_gate = w_in_b[:, :, 3 * wdt + n_rw:]
    ffn1 = (bf(ffn1_wg), bf(ffn1_wu), bf(ffn1_wd), vec(ln1_g), vec(ln1_b))
    ffn2 = (bf(ffn2_wg), bf(ffn2_wu), bf(ffn2_wd), vec(ln4_g), vec(ln4_b))

    bq, cq, a_re, a_im = _s5_discretise(s5_lam_re, s5_lam_im, s5_log_dt, s5_b_re, s5_b_im, s5_c_re, s5_c_im)
    wa_bd, wx_bd = bf(_block_diag(lru_wa)), bf(_block_diag(lru_wx))

    kh_n = LANES // (bsz * heads)
    assert kh_n == 2 and HEAD_DIM % kh_n == 0, "lane layout b*16 + kh*8 + h needs batch * heads * 2 == 128"
    kp_n = HEAD_DIM // kh_n
    key_perm = np.array([h_ * HEAD_DIM + kh_ * kp_n + kp_
                         for kp_ in range(kp_n) for kh_ in range(kh_n) for h_ in range(heads)])
    val_perm = np.array([h_ * HEAD_DIM + v_ for v_ in range(HEAD_DIM) for h_ in range(heads)])
    val_rep_perm = np.array([h_ * HEAD_DIM + v_
                             for v_ in range(HEAD_DIM) for _ in range(kh_n) for h_ in range(heads)])

    def rw_layout(p):
        return jnp.concatenate([p[..., 0:wdt][..., key_perm], p[..., wdt:2 * wdt][..., key_perm],
                                p[..., 2 * wdt:3 * wdt][..., val_rep_perm], p[..., 2 * wdt:3 * wdt][..., val_perm],
                                p[..., 3 * wdt:]], axis=-1)

    w_rw_p = rw_layout(w_rw)
    mu_p = vec(rw_layout(rw_mu))
    zeros_dr = jnp.zeros((depth, RW_A_RANK, wdt), BF16)
    zeros_ar = jnp.zeros((depth, RW_DECAY_RANK, wdt), BF16)
    wup_pad = jnp.concatenate([bf(rw_w_up)[:, :, key_perm], zeros_dr], axis=1)
    aup_pad = jnp.concatenate([zeros_ar, bf(rw_a_up)[:, :, key_perm]], axis=1)
    gup_p = bf(rw_g_up)[:, :, val_perm]
    w0_p, a0_p = vec(rw_w0[:, key_perm]), vec(rw_a0[:, key_perm])
    kk_p, ka_p = vec(rw_k_k[:, key_perm]), vec(rw_k_a[:, key_perm])
    rk_p = vec(rw_r_k.reshape(depth, wdt)[:, key_perm])
    gn_g_p, gn_b_p = vec(rw_ln_g[:, val_perm]), vec(rw_ln_b[:, val_perm])
    rw_o = bf(rw_w_out)[:, val_perm, :]
    head_of = jnp.arange(wdt) % heads
    ones_h = (head_of[:, None] == head_of[None, :]).astype(BF16)
    lane = jnp.arange(LANES)
    grp = kh_n * heads
    sel_b = ((lane[None, :] < grp) & (lane[None, :] % heads == head_of[:, None])).astype(BF16)
    sel_k = ((lane[None, :] >= grp) & (lane[None, :] < 2 * grp)
             & (lane[None, :] % heads == head_of[:, None])).astype(BF16)

    kv = _kv_proj(mem.reshape(-1, dm), bf(xa_wkv), 512)
    wq_b, wo_b = bf(xa_wq), bf(xa_wo)
    glu1, glu2, lru_o, mix = bf(s5_glu_w1), bf(s5_glu_w2), bf(lru_w_out), bf(mix_w_out)

    tm = _tile(seq, 512)

    h = x
    for l in range(depth):
        h = _ffn_ln(h, *ffn1, l, alpha, tm, bsz, batch_major_in=(l == 0))

        h_rows = h.reshape(t, dm)
        y_s5 = _s5_branch(h_rows, w_s5, bq, cq, a_re, a_im, vec(s5_d), l, _tile(t, 512))
        y_lru = _lru_branch(h_rows, w_lx, w_ly, lru_conv_w, vec(lru_conv_b), wa_bd, vec(lru_ba), wx_bd,
                            vec(lru_bx), vec(lru_lambda), l, _tile(t, 1024))
        k5, vs, g, bonus = _rw_prep(h, w_rw_p, mu_p, w0_p, wup_pad, a0_p, aup_pad, gup_p, kk_p, ka_p, rk_p,
                                    ones_h, sel_b, sel_k, l, bsz, _tile(seq, 512))
        y_rw = _rw_scan(k5, vs, bsz, heads, _tile(seq, 64))

        h = _merge(h, y_s5.reshape(seq, bsz * wdt), y_lru.reshape(seq, bsz * wdt), y_rw, bonus, g,
                   w_gate, glu1, glu2, lru_o, rw_o, mix,
                   gn_g_p, gn_b_p, ones_h, vec(ln2_g), vec(ln2_b), l, alpha, tm, bsz)
        h = _xattn(h, kv, wq_b, wo_b, vec(ln3_g), vec(ln3_b), l, alpha, bsz, _tile(seq, 512))
        h = _ffn_ln(h, *ffn2, l, alpha, tm, bsz, batch_major_out=(l == depth - 1))
    return h
```

```python
import functools
import math

import jax
import jax.numpy as jnp
import numpy as np
from jax import lax
from jax.experimental import pallas as pl
from jax.experimental.pallas import tpu as pltpu

F32 = jnp.float32
BF16 = jnp.bfloat16

LN_EPS = 1e-5
RW_GN_EPS = 64e-5
LRU_C = 8.0
LRU_CONV = 4
S5_GROUP = 16
S5_STATE = 64
HEAD_DIM = 64
XA_HEADS = 4
RW_DECAY_RANK = 64
RW_A_RANK = 64
RW_GATE_RANK = 128

SUBLANES = 8
LANES = 128
VMEM_LIMIT_BYTES = 56 * 1024 * 1024
RELAYOUT_ROWS = 64


def _cparams(*sem):
    return pltpu.CompilerParams(dimension_semantics=sem, vmem_limit_bytes=VMEM_LIMIT_BYTES)


def _const_spec(shape):
    nd = len(shape)
    return pl.BlockSpec(shape, lambda *_: (0,) * nd, pipeline_mode=pl.Buffered(1))


def _layer_spec(arr, l):
    tail = arr.shape[1:]
    nd = len(tail)
    return pl.BlockSpec((None,) + tail, lambda *_: (l,) + (0,) * nd, pipeline_mode=pl.Buffered(1))


def _layer_norm(y, g, b):
    mu = jnp.mean(y, -1, keepdims=True)
    yc = y - mu
    var = jnp.mean(yc * yc, -1, keepdims=True)
    return yc * lax.rsqrt(var + LN_EPS) * g + b


def _mm(a, b):
    return jnp.dot(a, b, preferred_element_type=F32)


def _segsum(x, ones_bd):
    hi = x.astype(BF16)
    lo = (x - hi.astype(F32)).astype(BF16)
    return _mm(hi, ones_bd) + _mm(lo, ones_bd)


def _softplus(x):
    return jnp.maximum(x, 0.0) + jnp.log1p(jnp.exp(-jnp.abs(x)))


def _ffn_kernel(x_ref, wg_ref, wu_ref, wd_ref, g_ref, b_ref, o_ref, *, alpha, f_chunk):
    x = x_ref[...]
    xb = x.astype(BF16)
    d_ff = wg_ref.shape[1]
    acc = jnp.zeros(x.shape, F32)
    for c0 in range(0, d_ff, f_chunk):
        hg = _mm(xb, wg_ref[:, c0:c0 + f_chunk])
        hu = _mm(xb, wu_ref[:, c0:c0 + f_chunk])
        h = hg * jax.nn.sigmoid(hg) * hu
        acc = acc + _mm(h.astype(BF16), wd_ref[c0:c0 + f_chunk, :])
    o_ref[...] = _layer_norm(alpha * x + 0.5 * acc, g_ref[...], b_ref[...])


def _seq_spec(tm, n):
    return pl.BlockSpec((tm, n), lambda i, b: (i, b))


def _batch_major_spec(tm, n):
    return pl.BlockSpec((None, tm, n), lambda i, b: (b, i, 0))


def _ffn_ln(x, wg, wu, wd, g, b, l, alpha, tm, bsz, batch_major_in=False, batch_major_out=False):
    d = wg.shape[1]
    seq = x.shape[1] if batch_major_in else x.shape[0]
    out_shape = (bsz, seq, d) if batch_major_out else (seq, bsz * d)
    return pl.pallas_call(
        functools.partial(_ffn_kernel, alpha=alpha, f_chunk=256),
        grid=(seq // tm, bsz),
        in_specs=[_batch_major_spec(tm, d) if batch_major_in else _seq_spec(tm, d),
                  _layer_spec(wg, l), _layer_spec(wu, l), _layer_spec(wd, l),
                  _layer_spec(g, l), _layer_spec(b, l)],
        out_specs=_batch_major_spec(tm, d) if batch_major_out else _seq_spec(tm, d),
        out_shape=jax.ShapeDtypeStruct(out_shape, F32),
        compiler_params=_cparams("parallel", "parallel"),
        name="ffn_ln",
    )(x, wg, wu, wd, g, b)


def _s5_kernel(x_ref, w_ref, bq_ref, cq_ref, are_ref, aim_ref, d_ref, o_ref,
               st_ref, carry_ref, u_ref):
    rows = x_ref.shape[0]
    nq = bq_ref.shape[0]
    half = bq_ref.shape[2] // 2
    cw = bq_ref.shape[1]

    @pl.when(pl.program_id(0) == 0)
    def _():
        carry_ref[...] = jnp.zeros_like(carry_ref)

    u = _mm(x_ref[...].astype(BF16), w_ref[...])
    u_ref[...] = u
    ub = u.astype(BF16)
    for q in range(nq):
        st_ref[:, 2 * half * q:2 * half * (q + 1)] = _mm(ub[:, cw * q:cw * (q + 1)], bq_ref[q])

    def step(t, s):
        r0 = pl.multiple_of(t * SUBLANES, SUBLANES)
        parts = []
        for q in range(nq):
            re = s[:, 2 * half * q:2 * half * q + half]
            im = s[:, 2 * half * q + half:2 * half * (q + 1)]
            ar = are_ref[:, half * q:half * (q + 1)]
            ai = aim_ref[:, half * q:half * (q + 1)]
            bre = st_ref[pl.ds(r0, SUBLANES), 2 * half * q:2 * half * q + half]
            bim = st_ref[pl.ds(r0, SUBLANES), 2 * half * q + half:2 * half * (q + 1)]
            parts.append(ar * re - ai * im + bre)
            parts.append(ar * im + ai * re + bim)
        s_new = jnp.concatenate(parts, axis=1)
        st_ref[pl.ds(r0, SUBLANES), :] = s_new
        return s_new

    carry_ref[...] = lax.fori_loop(0, rows // SUBLANES, step, carry_ref[...])

    for q in range(nq):
        y = _mm(st_ref[:, 2 * half * q:2 * half * (q + 1)].astype(BF16), cq_ref[q])
        y = y + d_ref[:, cw * q:cw * (q + 1)] * u_ref[:, cw * q:cw * (q + 1)]
        o_ref[:, cw * q:cw * (q + 1)] = jax.nn.gelu(y)


def _s5_branch(x, w_s5, bq, cq, a_re, a_im, d, l, rows):
    t, dm = x.shape
    wdt = w_s5.shape[2]
    n_state = a_re.shape[2]
    return pl.pallas_call(
        _s5_kernel,
        grid=(t // rows,),
        in_specs=[pl.BlockSpec((rows, dm), lambda i: (i, 0)),
                  _layer_spec(w_s5, l), _layer_spec(bq, l), _layer_spec(cq, l),
                  _layer_spec(a_re, l), _layer_spec(a_im, l), _layer_spec(d, l)],
        out_specs=pl.BlockSpec((rows, wdt), lambda i: (i, 0)),
        out_shape=jax.ShapeDtypeStruct((t, wdt), F32),
        scratch_shapes=[pltpu.VMEM((rows, 2 * n_state), F32),
                        pltpu.VMEM((SUBLANES, 2 * n_state), F32),
                        pltpu.VMEM((rows, wdt), F32)],
        compiler_params=_cparams("arbitrary"),
        name="s5_branch",
    )(x, w_s5, bq, cq, a_re, a_im, d)


def _lru_kernel(x_ref, wlx_ref, wly_ref, cw_ref, cb_ref, wa_ref, ba_ref, wx_ref, bx_ref,
                lam_ref, o_ref, hist_ref, h_ref, a_ref, b_ref):
    rows = x_ref.shape[0]
    hist_rows = hist_ref.shape[0]

    @pl.when(pl.program_id(0) == 0)
    def _():
        hist_ref[...] = jnp.zeros_like(hist_ref)
        h_ref[...] = jnp.zeros_like(h_ref)

    xb = x_ref[...].astype(BF16)
    zx = _mm(xb, wlx_ref[...])
    zy = _mm(xb, wly_ref[...])

    ext = jnp.concatenate([hist_ref[...], zx], axis=0)
    hist_ref[...] = zx[rows - hist_rows:, :]
    xc = cb_ref[...] + cw_ref[LRU_CONV - 1:LRU_CONV, :] * zx
    for j in range(1, LRU_CONV):
        off = hist_rows - SUBLANES * j
        xc = xc + cw_ref[LRU_CONV - 1 - j:LRU_CONV - j, :] * ext[off:off + rows, :]

    xcb = xc.astype(BF16)
    gate_r = jax.nn.sigmoid(_mm(xcb, wa_ref[...]) + ba_ref[...])
    gate_i = jax.nn.sigmoid(_mm(xcb, wx_ref[...]) + bx_ref[...])
    log_a = -LRU_C * gate_r * _softplus(-lam_ref[...])
    a = jnp.exp(log_a)
    a_ref[...] = a
    b_ref[...] = jnp.sqrt(-jnp.tanh(log_a) * (1.0 + a * a)) * gate_i * xc

    def step(t, h):
        r0 = pl.multiple_of(t * SUBLANES, SUBLANES)
        h = a_ref[pl.ds(r0, SUBLANES), :] * h + b_ref[pl.ds(r0, SUBLANES), :]
        b_ref[pl.ds(r0, SUBLANES), :] = h
        return h

    h_ref[...] = lax.fori_loop(0, rows // SUBLANES, step, h_ref[...], unroll=8)
    o_ref[...] = b_ref[...] * jax.nn.gelu(zy)


def _lru_branch(x, w_lx, w_ly, conv_w, conv_b, wa_bd, ba, wx_bd, bx, lam, l, rows):
    t, dm = x.shape
    wdt = w_lx.shape[2]
    params = (w_lx, w_ly, conv_w, conv_b, wa_bd, ba, wx_bd, bx, lam)
    return pl.pallas_call(
        _lru_kernel,
        grid=(t // rows,),
        in_specs=[pl.BlockSpec((rows, dm), lambda i: (i, 0))] + [_layer_spec(p, l) for p in params],
        out_specs=pl.BlockSpec((rows, wdt), lambda i: (i, 0)),
        out_shape=jax.ShapeDtypeStruct((t, wdt), F32),
        scratch_shapes=[pltpu.VMEM(((LRU_CONV - 1) * SUBLANES, wdt), F32),
                        pltpu.VMEM((SUBLANES, wdt), F32),
                        pltpu.VMEM((rows, wdt), F32),
                        pltpu.VMEM((rows, wdt), F32)],
        compiler_params=_cparams("arbitrary"),
        name="lru_branch",
    )(x, *params)


def _swap_tile_and_lane_group(tiles, grp):
    n = len(tiles)
    lane_g = lax.broadcasted_iota(jnp.int32, tiles[0].shape, 1) // grp
    k = 1
    while k < n:
        high = (lane_g & k) != 0
        nxt = list(tiles)
        for p in range(n):
            if p & k == 0:
                lo_t, hi_t = tiles[p], tiles[p + k]
                nxt[p] = jnp.where(high, pltpu.roll(hi_t, k * grp, 1), lo_t)
                nxt[p + k] = jnp.where(high, hi_t, pltpu.roll(lo_t, LANES - k * grp, 1))
        tiles = nxt
        k *= 2
    return tiles


def _rw_prep_kernel(x_ref, w_ref, mu_ref, w0_ref, wup_ref, a0_ref, aup_ref, gup_ref,
                    kk_ref, ka_ref, rk_ref, ones_ref, sel_b_ref, sel_k_ref,
                    k5_ref, vs_ref, g_ref, bonus_ref, prev_ref):
    tt = x_ref.shape[0]
    cols = w_ref.shape[1]
    wdt = w0_ref.shape[1]
    lora0 = 5 * wdt
    lora1 = lora0 + wup_ref.shape[0]
    b = pl.program_id(1)

    @pl.when(pl.program_id(0) == 0)
    def _():
        prev_ref[b] = jnp.zeros(prev_ref.shape[1:], F32)

    z = _mm(x_ref[...].astype(BF16), w_ref[...])
    first_row = lax.broadcasted_iota(jnp.int32, (tt, cols), 0) == 0
    shifted = jnp.where(first_row, prev_ref[b, 0:1, :], pltpu.roll(z, 1, 0))
    prev_ref[b, 0:1, :] = z[tt - 1:tt, :]
    z = z + mu_ref[...] * (shifted - z)

    r = z[:, 0:wdt]
    k = z[:, wdt:2 * wdt]
    v_rep = z[:, 2 * wdt:4 * wdt]
    v = z[:, 4 * wdt:5 * wdt]
    lora = z[:, lora0:lora1]
    gd = z[:, lora1:]

    w_log = -_softplus(-(w0_ref[...] + _mm(jnp.tanh(lora).astype(BF16), wup_ref[...]))) - 0.5
    decay = jnp.exp(-jnp.exp(w_log))
    a = jax.nn.sigmoid(a0_ref[...] + _mm(lora.astype(BF16), aup_ref[...]))
    g_ref[...] = _mm(jax.nn.sigmoid(gd).astype(BF16), gup_ref[...])

    ones_h = ones_ref[...]
    kk = k * kk_ref[...]
    kk = kk * lax.rsqrt(_segsum(kk * kk, ones_h) + 1e-12)
    k = k * (1.0 + (a - 1.0) * ka_ref[...])
    kka = kk * a

    k5_ref[:, 0:wdt] = -kk
    k5_ref[:, wdt:2 * wdt] = decay
    k5_ref[:, 2 * wdt:3 * wdt] = kka
    k5_ref[:, 3 * wdt:4 * wdt] = k
    k5_ref[:, 4 * wdt:5 * wdt] = decay * r
    vs_ref[:, 0:2 * wdt] = v_rep
    vs_ref[:, 2 * wdt:2 * wdt + LANES] = _segsum(kka * r, sel_b_ref[...]) + _segsum(k * r, sel_k_ref[...])
    bonus_ref[...] = _segsum(r * k * rk_ref[...], ones_h) * v


def _rw_prep(x, w_rw, mu, w0, wup_pad, a0, aup_pad, gup, k_k, k_a, r_k, ones_h, sel_b, sel_k, l, bsz, tt):
    seq = x.shape[0]
    dm = x.shape[1] // bsz
    wdt = w0.shape[2]
    cols = w_rw.shape[2]
    params = (w_rw, mu, w0, wup_pad, a0, aup_pad, gup, k_k, k_a, r_k)
    per_batch = lambda n: _seq_spec(tt, n)
    widths = (5 * wdt, 2 * wdt + LANES, wdt, wdt)
    return pl.pallas_call(
        _rw_prep_kernel,
        grid=(seq // tt, bsz),
        in_specs=[per_batch(dm)] + [_layer_spec(p, l) for p in params]
                 + [_const_spec(ones_h.shape), _const_spec(sel_b.shape), _const_spec(sel_k.shape)],
        out_specs=[per_batch(n) for n in widths],
        out_shape=[jax.ShapeDtypeStruct((seq, bsz * n), F32) for n in widths],
        scratch_shapes=[pltpu.VMEM((bsz, SUBLANES, cols), F32)],
        compiler_params=_cparams("arbitrary", "arbitrary"),
        name="rw_prep",
    )(x, *params, ones_h, sel_b, sel_k)


def _rw_scan_kernel(k5_ref, vs_ref, y_ref, st_ref, k5p_ref, v2x_ref, ypl_ref, *, bsz, heads):
    tt = k5_ref.shape[0]
    kp_n = st_ref.shape[0]
    hd = st_ref.shape[1]
    wdt = hd * heads
    grp = LANES // bsz
    k5w = k5_ref.shape[1] // bsz
    vsw = vs_ref.shape[1] // bsz
    vt = hd // SUBLANES
    @pl.when(pl.program_id(0) == 0)
    def _():
        st_ref[...] = jnp.zeros_like(st_ref)

    rc = min(tt, RELAYOUT_ROWS)

    def batch_tiles(src_ref, width, blk, r0):
        return [src_ref[r0:r0 + rc, pl.ds(pl.multiple_of(b * width + blk * LANES, LANES), LANES)]
                for b in range(bsz)]

    def k5_block(blk, carry):
        for r0 in range(0, tt, rc):
            for i, tile in enumerate(_swap_tile_and_lane_group(batch_tiles(k5_ref, k5w, blk, r0), grp)):
                k5p_ref[blk * bsz + i, r0:r0 + rc, :] = tile
        return carry

    def vs_block(blk, carry):
        for r0 in range(0, tt, rc):
            for i, tile in enumerate(_swap_tile_and_lane_group(batch_tiles(vs_ref, vsw, blk, r0), grp)):
                v2x_ref[blk, r0:r0 + rc, i, :] = tile
        return carry

    lax.fori_loop(0, k5w // LANES, k5_block, 0)
    lax.fori_loop(0, vsw // LANES, vs_block, 0)

    low_half = (lax.broadcasted_iota(jnp.int32, (hd, LANES), 1) // heads) % 2 == 0

    def fold(p):
        return p + jnp.where(low_half, pltpu.roll(p, LANES - heads, 1), pltpu.roll(p, heads, 1))

    def row(j, kp, t):
        return k5p_ref[j * kp_n + kp, pl.ds(t, 1), :]

    def step(t, carry):
        vv = jnp.concatenate([v2x_ref[m, t] for m in range(vt)], axis=0)
        b_r = v2x_ref[vt, t, 0:1, :]
        k_r = v2x_ref[vt, t, 1:2, :]
        sa = jnp.zeros((hd, LANES), F32)
        yp = jnp.zeros((hd, LANES), F32)
        for kp in range(kp_n):
            s_k = st_ref[kp]
            sa = sa + s_k * row(0, kp, t)
            yp = yp + s_k * row(4, kp, t)
        sa = fold(sa)
        yp = fold(yp)
        for kp in range(kp_n):
            st_ref[kp] = st_ref[kp] * row(1, kp, t) + sa * row(2, kp, t) + vv * row(3, kp, t)
        y = yp + sa * b_r + vv * k_r
        for v in range(hd):
            ypl_ref[v, pl.ds(t, 1), :] = y[v:v + 1, :]
        return carry

    lax.fori_loop(0, tt, step, 0)

    pair_lo = lax.broadcasted_iota(jnp.int32, (rc, LANES), 1) % grp < heads

    def out_block(vb, carry):
        for r0 in range(0, tt, rc):
            pairs = []
            for m in range(bsz):
                even = ypl_ref[vb * 2 * bsz + 2 * m, r0:r0 + rc, :]
                odd = ypl_ref[vb * 2 * bsz + 2 * m + 1, r0:r0 + rc, :]
                pairs.append(jnp.where(pair_lo, even, pltpu.roll(odd, heads, 1)))
            for b, tile in enumerate(_swap_tile_and_lane_group(pairs, grp)):
                y_ref[r0:r0 + rc, pl.ds(pl.multiple_of(b * wdt + vb * LANES, LANES), LANES)] = tile
        return carry

    lax.fori_loop(0, wdt // LANES, out_block, 0)


def _rw_scan(k5, vs, bsz, heads, tt):
    seq = k5.shape[0]
    hd = HEAD_DIM
    wdt = hd * heads
    n_k5 = k5.shape[1] // LANES
    n_vs = vs.shape[1] // bsz // LANES
    return pl.pallas_call(
        functools.partial(_rw_scan_kernel, bsz=bsz, heads=heads),
        grid=(seq // tt,),
        in_specs=[pl.BlockSpec((tt, k5.shape[1]), lambda i: (i, 0)),
                  pl.BlockSpec((tt, vs.shape[1]), lambda i: (i, 0))],
        out_specs=pl.BlockSpec((tt, bsz * wdt), lambda i: (i, 0)),
        out_shape=jax.ShapeDtypeStruct((seq, bsz * wdt), F32),
        scratch_shapes=[pltpu.VMEM((n_k5 // 5, hd, LANES), F32),
                        pltpu.VMEM((n_k5, tt, LANES), F32),
                        pltpu.VMEM((n_vs, tt, bsz, LANES), F32),
                        pltpu.VMEM((hd, tt, LANES), F32)],
        compiler_params=_cparams("arbitrary"),
        name="rw_scan",
    )(k5, vs)


def _merge_kernel(x_ref, ys5_ref, ylru_ref, yrw_ref, bonus_ref, g_ref,
                  wgate_ref, glu1_ref, glu2_ref, lruo_ref, rwo_ref, mix_ref,
                  gng_ref, gnb_ref, ones_ref, lng_ref, lnb_ref, o_ref, *, alpha):
    x = x_ref[...]
    dm = x.shape[1]
    gates = jax.nn.sigmoid(_mm(x.astype(BF16), wgate_ref[...]))

    ys5 = ys5_ref[...].astype(BF16)
    y_s5 = _mm(ys5, glu1_ref[...]) * jax.nn.sigmoid(_mm(ys5, glu2_ref[...]))
    y_lru = _mm(ylru_ref[...].astype(BF16), lruo_ref[...])

    ones_bd = ones_ref[...]
    inv_n = 1.0 / HEAD_DIM
    y = yrw_ref[...]
    yc = y - _segsum(y, ones_bd) * inv_n
    var = _segsum(yc * yc, ones_bd) * inv_n
    y = yc * lax.rsqrt(var + RW_GN_EPS) * gng_ref[...] + gnb_ref[...]
    y = (y + bonus_ref[...]) * g_ref[...]
    y_rw = _mm(y.astype(BF16), rwo_ref[...])

    merged = (gates[:, 0:dm] * y_s5 + gates[:, dm:2 * dm] * y_lru + gates[:, 2 * dm:3 * dm] * y_rw)
    o_ref[...] = _layer_norm(alpha * x + _mm(merged.astype(BF16), mix_ref[...]),
                             lng_ref[...], lnb_ref[...])


def _merge(x, ys5, ylru, yrw, bonus, g, w_gate, glu1, glu2, lru_o, rw_o, mix, gn_g, gn_b, ones_bd,
           ln_g, ln_b, l, alpha, tm, bsz):
    seq = x.shape[0]
    dm = x.shape[1] // bsz
    wdt = ys5.shape[1] // bsz
    params = (w_gate, glu1, glu2, lru_o, rw_o, mix, gn_g, gn_b)
    return pl.pallas_call(
        functools.partial(_merge_kernel, alpha=alpha),
        grid=(seq // tm, bsz),
        in_specs=[_seq_spec(tm, dm)] + [_seq_spec(tm, wdt)] * 5 + [_layer_spec(p, l) for p in params]
                 + [_const_spec(ones_bd.shape), _layer_spec(ln_g, l), _layer_spec(ln_b, l)],
        out_specs=_seq_spec(tm, dm),
        out_shape=jax.ShapeDtypeStruct((seq, bsz * dm), F32),
        compiler_params=_cparams("parallel", "parallel"),
        name="merge_mix_ln",
    )(x, ys5, ylru, yrw, bonus, g, *params, ones_bd, ln_g, ln_b)


def _kv_kernel(mem_ref, w_ref, o_ref):
    o_ref[...] = _mm(mem_ref[...].astype(BF16), w_ref[...]).astype(BF16)


def _kv_proj(mem2, wkv, tn):
    rows, dm = mem2.shape
    depth, _, n = wkv.shape
    return pl.pallas_call(
        _kv_kernel,
        grid=(depth, n // tn),
        in_specs=[pl.BlockSpec((rows, dm), lambda l, j: (0, 0)),
                  pl.BlockSpec((None, dm, tn), lambda l, j: (l, 0, j))],
        out_specs=pl.BlockSpec((None, rows, tn), lambda l, j: (l, 0, j)),
        out_shape=jax.ShapeDtypeStruct((depth, rows, n), BF16),
        compiler_params=_cparams("parallel", "parallel"),
        name="xa_kv_proj",
    )(mem2, wkv)


def _xattn_kernel(x_ref, k_ref, v_ref, wq_ref, wo_ref, g_ref, b_ref, o_ref, *, alpha, heads):
    x = x_ref[...]
    dm = x.shape[1]
    hd = dm // heads
    q = _mm(x.astype(BF16), wq_ref[...]).astype(BF16)
    outs = []
    for h in range(heads):
        sl = slice(h * hd, (h + 1) * hd)
        s = lax.dot_general(q[:, sl], k_ref[:, sl], (((1,), (1,)), ((), ())),
                            preferred_element_type=F32) * (hd ** -0.5)
        e = jnp.exp(s - jnp.max(s, -1, keepdims=True))
        p = e / jnp.sum(e, -1, keepdims=True)
        outs.append(_mm(p.astype(BF16), v_ref[:, sl]))
    o = jnp.concatenate(outs, axis=1)
    o_ref[...] = _layer_norm(alpha * x + _mm(o.astype(BF16), wo_ref[...]), g_ref[...], b_ref[...])


def _xattn(x, kv, wq, wo, g, b, l, alpha, bsz, ts):
    s = x.shape[0]
    dm = x.shape[1] // bsz
    m = kv.shape[1] // bsz
    seq = pl.BlockSpec((ts, dm), lambda bi, i: (i, bi))
    return pl.pallas_call(
        functools.partial(_xattn_kernel, alpha=alpha, heads=XA_HEADS),
        grid=(bsz, s // ts),
        in_specs=[seq,
                  pl.BlockSpec((None, m, dm), lambda bi, i: (l, bi, 0)),
                  pl.BlockSpec((None, m, dm), lambda bi, i: (l, bi, 1)),
                  _layer_spec(wq, l), _layer_spec(wo, l), _layer_spec(g, l), _layer_spec(b, l)],
        out_specs=seq,
        out_shape=jax.ShapeDtypeStruct((s, bsz * dm), F32),
        compiler_params=_cparams("parallel", "parallel"),
        name="xattn_ln",
    )(x, kv, kv, wq, wo, g, b)


def _block_diag(w):
    depth, h, n, _ = w.shape
    eye = jnp.eye(h, dtype=w.dtype)
    return (w[:, :, :, None, :] * eye[None, :, None, :, None]).reshape(depth, h * n, h * n)


def _s5_discretise(lam_re, lam_im, log_dt, b_re, b_im, c_re, c_im):
    depth, g, p = lam_re.shape
    c = b_re.shape[-1]
    gb = LANES // c
    nq = g // gb
    dt = jnp.exp(log_dt)[..., None]
    mag = jnp.exp(lam_re * dt)
    a_re = mag * jnp.cos(lam_im * dt)
    a_im = mag * jnp.sin(lam_im * dt)
    den = lam_re * lam_re + lam_im * lam_im
    co_re = ((a_re - 1.0) * lam_re + a_im * lam_im) / den
    co_im = (a_im * lam_re - (a_re - 1.0) * lam_im) / den
    bb_re = co_re[..., None] * b_re - co_im[..., None] * b_im
    bb_im = co_re[..., None] * b_im + co_im[..., None] * b_re
    eye = jnp.eye(gb, dtype=F32)

    def b_layout(w):
        w = jnp.swapaxes(w.reshape(depth, nq, gb, p, c), 3, 4)
        return (w[:, :, :, :, None, :] * eye[None, None, :, None, :, None]).reshape(depth, nq, gb * c, gb * p)

    def c_layout(w):
        w = jnp.swapaxes(w.reshape(depth, nq, gb, c, p), 3, 4)
        return (w[:, :, :, :, None, :] * eye[None, None, :, None, :, None]).reshape(depth, nq, gb * p, gb * c)

    bq = jnp.concatenate([b_layout(bb_re), b_layout(bb_im)], axis=-1).astype(BF16)
    cq = jnp.concatenate([c_layout(c_re), c_layout(-c_im)], axis=-2).astype(BF16)
    return bq, cq, a_re.reshape(depth, 1, g * p), a_im.reshape(depth, 1, g * p)


def _tile(n, target):
    return min(n, target)


def kernel(x, mem, ffn1_wg, ffn1_wu, ffn1_wd, ln1_g, ln1_b, w_in, s5_lam_re, s5_lam_im, s5_log_dt, s5_b_re, s5_b_im, s5_c_re, s5_c_im, s5_d, s5_glu_w1, s5_glu_w2, lru_conv_w, lru_conv_b, lru_wa, lru_ba, lru_wx, lru_bx, lru_lambda, lru_w_out, rw_mu, rw_w0, rw_w_up, rw_a0, rw_a_up, rw_g_up, rw_k_k, rw_k_a, rw_r_k, rw_ln_g, rw_ln_b, rw_w_out, mix_w_out, ln2_g, ln2_b, xa_wq, xa_wkv, xa_wo, ln3_g, ln3_b, ffn2_wg, ffn2_wu, ffn2_wd, ln4_g, ln4_b):
    bsz, seq, dm = x.shape
    depth = w_in.shape[0]
    wdt = s5_d.shape[1]
    heads = wdt // HEAD_DIM
    alpha = (2 * depth) ** 0.25
    assert bsz == SUBLANES, "one time step must be one aligned 8-row group"
    t = bsz * seq

    bf = lambda w: w.astype(BF16)
    vec = lambda p: p.reshape(depth, 1, -1)

    w_in_b = bf(w_in)
    w_s5, w_lx, w_ly = w_in_b[:, :, 0:wdt], w_in_b[:, :, wdt:2 * wdt], w_in_b[:, :, 2 * wdt:3 * wdt]
    n_rw = rw_mu.shape[1]
    w_rw = w_in_b[:, :, 3 * wdt:3 * wdt + n_rw]
    w_gate = w_in_b[:, :, 3 * wdt + n_rw:]
    ffn1 = (bf(ffn1_wg), bf(ffn1_wu), bf(ffn1_wd), vec(ln1_g), vec(ln1_b))
    ffn2 = (bf(ffn2_wg), bf(ffn2_wu), bf(ffn2_wd), vec(ln4_g), vec(ln4_b))

    bq, cq, a_re, a_im = _s5_discretise(s5_lam_re, s5_lam_im, s5_log_dt, s5_b_re, s5_b_im, s5_c_re, s5_c_im)
    wa_bd, wx_bd = bf(_block_diag(lru_wa)), bf(_block_diag(lru_wx))

    kh_n = LANES // (bsz * heads)
    assert kh_n == 2 and HEAD_DIM % kh_n == 0, "lane layout b*16 + kh*8 + h needs batch * heads * 2 == 128"
    kp_n = HEAD_DIM // kh_n
    key_perm = np.array([h_ * HEAD_DIM + kh_ * kp_n + kp_
                         for kp_ in range(kp_n) for kh_ in range(kh_n) for h_ in range(heads)])
    val_perm = np.array([h_ * HEAD_DIM + v_ for v_ in range(HEAD_DIM) for h_ in range(heads)])
    val_rep_perm = np.array([h_ * HEAD_DIM + v_
                             for v_ in range(HEAD_DIM) for _ in range(kh_n) for h_ in range(heads)])

    def rw_layout(p):
        return jnp.concatenate([p[..., 0:wdt][..., key_perm], p[..., wdt:2 * wdt][..., key_perm],
                                p[..., 2 * wdt:3 * wdt][..., val_rep_perm], p[..., 2 * wdt:3 * wdt][..., val_perm],
                                p[..., 3 * wdt:]], axis=-1)

    w_rw_p = rw_layout(w_rw)
    mu_p = vec(rw_layout(rw_mu))
    zeros_dr = jnp.zeros((depth, RW_A_RANK, wdt), BF16)
    zeros_ar = jnp.zeros((depth, RW_DECAY_RANK, wdt), BF16)
    wup_pad = jnp.concatenate([bf(rw_w_up)[:, :, key_perm], zeros_dr], axis=1)
    aup_pad = jnp.concatenate([zeros_ar, bf(rw_a_up)[:, :, key_perm]], axis=1)
    gup_p = bf(rw_g_up)[:, :, val_perm]
    w0_p, a0_p = vec(rw_w0[:, key_perm]), vec(rw_a0[:, key_perm])
    kk_p, ka_p = vec(rw_k_k[:, key_perm]), vec(rw_k_a[:, key_perm])
    rk_p = vec(rw_r_k.reshape(depth, wdt)[:, key_perm])
    gn_g_p, gn_b_p = vec(rw_ln_g[:, val_perm]), vec(rw_ln_b[:, val_perm])
    rw_o = bf(rw_w_out)[:, val_perm, :]
    head_of = jnp.arange(wdt) % heads
    ones_h = (head_of[:, None] == head_of[None, :]).astype(BF16)
    lane = jnp.arange(LANES)
    grp = kh_n * heads
    sel_b = ((lane[None, :] < grp) & (lane[None, :] % heads == head_of[:, None])).astype(BF16)
    sel_k = ((lane[None, :] >= grp) & (lane[None, :] < 2 * grp)
             & (lane[None, :] % heads == head_of[:, None])).astype(BF16)

    kv = _kv_proj(mem.reshape(-1, dm), bf(xa_wkv), 512)
    wq_b, wo_b = bf(xa_wq), bf(xa_wo)
    glu1, glu2, lru_o, mix = bf(s5_glu_w1), bf(s5_glu_w2), bf(lru_w_out), bf(mix_w_out)

    tm = _tile(seq, 512)

    h = x
    for l in range(depth):
        h = _ffn_ln(h, *ffn1, l, alpha, tm, bsz, batch_major_in=(l == 0))

        h_rows = h.reshape(t, dm)
        y_s5 = _s5_branch(h_rows, w_s5, bq, cq, a_re, a_im, vec(s5_d), l, _tile(t, 512))
        y_lru = _lru_branch(h_rows, w_lx, w_ly, lru_conv_w, vec(lru_conv_b), wa_bd, vec(lru_ba), wx_bd,
                            vec(lru_bx), vec(lru_lambda), l, _tile(t, 1024))
        k5, vs, g, bonus = _rw_prep(h, w_rw_p, mu_p, w0_p, wup_pad, a0_p, aup_pad, gup_p, kk_p, ka_p, rk_p,
                                    ones_h, sel_b, sel_k, l, bsz, _tile(seq, 512))
        y_rw = _rw_scan(k5, vs, bsz, heads, _tile(seq, 64))

        h = _merge(h, y_s5.reshape(seq, bsz * wdt), y_lru.reshape(seq, bsz * wdt), y_rw, bonus, g,
                   w_gate, glu1, glu2, lru_o, rw_o, mix,
                   gn_g_p, gn_b_p, ones_h, vec(ln2_g), vec(ln2_b), l, alpha, tm, bsz)
        h = _xattn(h, kv, wq_b, wo_b, vec(ln3_g), vec(ln3_b), l, alpha, bsz, _tile(seq, 512))
        h = _ffn_ln(h, *ffn2, l, alpha, tm, bsz, batch_major_out=(l == depth - 1))
    return h
```

```python
import functools
import math

import jax
import jax.numpy as jnp
import numpy as np
from jax import lax
from jax.experimental import pallas as pl
from jax.experimental.pallas import tpu as pltpu

F32 = jnp.float32
BF16 = jnp.bfloat16

LN_EPS = 1e-5
RW_GN_EPS = 64e-5
LRU_C = 8.0
LRU_CONV = 4
S5_GROUP = 16
S5_STATE = 64
HEAD_DIM = 64
XA_HEADS = 4
RW_DECAY_RANK = 64
RW_A_RANK = 64
RW_GATE_RANK = 128

SUBLANES = 8
LANES = 128
VMEM_LIMIT_BYTES = 56 * 1024 * 1024
RELAYOUT_ROWS = 64


def _cparams(*sem):
    return pltpu.CompilerParams(dimension_semantics=sem, vmem_limit_bytes=VMEM_LIMIT_BYTES)


def _const_spec(shape):
    nd = len(shape)
    return pl.BlockSpec(shape, lambda *_: (0,) * nd, pipeline_mode=pl.Buffered(1))


def _layer_spec(arr, l):
    tail = arr.shape[1:]
    nd = len(tail)
    return pl.BlockSpec((None,) + tail, lambda *_: (l,) + (0,) * nd, pipeline_mode=pl.Buffered(1))


def _layer_norm(y, g, b):
    mu = jnp.mean(y, -1, keepdims=True)
    yc = y - mu
    var = jnp.mean(yc * yc, -1, keepdims=True)
    return yc * lax.rsqrt(var + LN_EPS) * g + b


def _mm(a, b):
    return jnp.dot(a, b, preferred_element_type=F32)


def _segsum(x, ones_bd):
    hi = x.astype(BF16)
    lo = (x - hi.astype(F32)).astype(BF16)
    return _mm(hi, ones_bd) + _mm(lo, ones_bd)


def _softplus(x):
    return jnp.maximum(x, 0.0) + jnp.log1p(jnp.exp(-jnp.abs(x)))


def _ffn_kernel(x_ref, wg_ref, wu_ref, wd_ref, g_ref, b_ref, o_ref, *maybe_ob_ref, alpha, f_chunk):
    x = x_ref[...]
    xb = x.astype(BF16)
    d_ff = wg_ref.shape[1]
    acc = jnp.zeros(x.shape, F32)
    for c0 in range(0, d_ff, f_chunk):
        hg = _mm(xb, wg_ref[:, c0:c0 + f_chunk])
        hu = _mm(xb, wu_ref[:, c0:c0 + f_chunk])
        h = hg * jax.nn.sigmoid(hg) * hu
        acc = acc + _mm(h.astype(BF16), wd_ref[c0:c0 + f_chunk, :])
    out = _layer_norm(alpha * x + 0.5 * acc, g_ref[...], b_ref[...])
    o_ref[...] = out
    for ob_ref in maybe_ob_ref:
        ob_ref[...] = out.astype(ob_ref.dtype)


def _seq_spec(tm, n):
    return pl.BlockSpec((tm, n), lambda i, b: (i, b))


def _batch_major_spec(tm, n):
    return pl.BlockSpec((None, tm, n), lambda i, b: (b, i, 0))


def _ffn_ln(x, wg, wu, wd, g, b, l, alpha, tm, bsz, batch_major_in=False, batch_major_out=False,
            with_bf16_copy=False):
    d = wg.shape[1]
    seq = x.shape[1] if batch_major_in else x.shape[0]
    out_shape = (bsz, seq, d) if batch_major_out else (seq, bsz * d)
    out_spec = _batch_major_spec(tm, d) if batch_major_out else _seq_spec(tm, d)
    out_specs, out_shapes = out_spec, jax.ShapeDtypeStruct(out_shape, F32)
    if with_bf16_copy:
        out_specs = [out_spec, out_spec]
        out_shapes = [out_shapes, jax.ShapeDtypeStruct(out_shape, BF16)]
    return pl.pallas_call(
        functools.partial(_ffn_kernel, alpha=alpha, f_chunk=256),
        grid=(seq // tm, bsz),
        in_specs=[_batch_major_spec(tm, d) if batch_major_in else _seq_spec(tm, d),
                  _layer_spec(wg, l), _layer_spec(wu, l), _layer_spec(wd, l),
                  _layer_spec(g, l), _layer_spec(b, l)],
        out_specs=out_specs,
        out_shape=out_shapes,
        compiler_params=_cparams("parallel", "parallel"),
        name="ffn_ln",
    )(x, wg, wu, wd, g, b)


def _s5_kernel(x_ref, w_ref, bq_ref, cq_ref, are_ref, aim_ref, d_ref, o_ref,
               st_ref, carry_ref, u_ref):
    rows = x_ref.shape[0]
    nq = bq_ref.shape[0]
    half = bq_ref.shape[2] // 2
    cw = bq_ref.shape[1]

    @pl.when(pl.program_id(0) == 0)
    def _():
        carry_ref[...] = jnp.zeros_like(carry_ref)

    u = _mm(x_ref[...].astype(BF16), w_ref[...])
    u_ref[...] = u
    ub = u.astype(BF16)
    for q in range(nq):
        st_ref[:, 2 * half * q:2 * half * (q + 1)] = _mm(ub[:, cw * q:cw * (q + 1)], bq_ref[q])

    def step(t, s):
        r0 = pl.multiple_of(t * SUBLANES, SUBLANES)
        parts = []
        for q in range(nq):
            re = s[:, 2 * half * q:2 * half * q + half]
            im = s[:, 2 * half * q + half:2 * half * (q + 1)]
            ar = are_ref[:, half * q:half * (q + 1)]
            ai = aim_ref[:, half * q:half * (q + 1)]
            bre = st_ref[pl.ds(r0, SUBLANES), 2 * half * q:2 * half * q + half]
            bim = st_ref[pl.ds(r0, SUBLANES), 2 * half * q + half:2 * half * (q + 1)]
            parts.append(ar * re - ai * im + bre)
            parts.append(ar * im + ai * re + bim)
        s_new = jnp.concatenate(parts, axis=1)
        st_ref[pl.ds(r0, SUBLANES), :] = s_new
        return s_new

    carry_ref[...] = lax.fori_loop(0, rows // SUBLANES, step, carry_ref[...])

    for q in range(nq):
        y = _mm(st_ref[:, 2 * half * q:2 * half * (q + 1)].astype(BF16), cq_ref[q])
        y = y + d_ref[:, cw * q:cw * (q + 1)] * u_ref[:, cw * q:cw * (q + 1)]
        o_ref[:, cw * q:cw * (q + 1)] = jax.nn.gelu(y).astype(o_ref.dtype)


def _s5_branch(x, w_s5, bq, cq, a_re, a_im, d, l, rows):
    t, dm = x.shape
    wdt = w_s5.shape[2]
    n_state = a_re.shape[2]
    return pl.pallas_call(
        _s5_kernel,
        grid=(t // rows,),
        in_specs=[pl.BlockSpec((rows, dm), lambda i: (i, 0)),
                  _layer_spec(w_s5, l), _layer_spec(bq, l), _layer_spec(cq, l),
                  _layer_spec(a_re, l), _layer_spec(a_im, l), _layer_spec(d, l)],
        out_specs=pl.BlockSpec((rows, wdt), lambda i: (i, 0)),
        out_shape=jax.ShapeDtypeStruct((t, wdt), BF16),
        scratch_shapes=[pltpu.VMEM((rows, 2 * n_state), F32),
                        pltpu.VMEM((SUBLANES, 2 * n_state), F32),
                        pltpu.VMEM((rows, wdt), F32)],
        compiler_params=_cparams("arbitrary"),
        name="s5_branch",
    )(x, w_s5, bq, cq, a_re, a_im, d)


def _lru_kernel(x_ref, wlx_ref, wly_ref, cw_ref, cb_ref, wa_ref, ba_ref, wx_ref, bx_ref,
                lam_ref, o_ref, hist_ref, h_ref, a_ref, b_ref):
    rows = x_ref.shape[0]
    hist_rows = hist_ref.shape[0]

    @pl.when(pl.program_id(0) == 0)
    def _():
        hist_ref[...] = jnp.zeros_like(hist_ref)
        h_ref[...] = jnp.zeros_like(h_ref)

    xb = x_ref[...].astype(BF16)
    zx = _mm(xb, wlx_ref[...])
    zy = _mm(xb, wly_ref[...])

    ext = jnp.concatenate([hist_ref[...], zx], axis=0)
    hist_ref[...] = zx[rows - hist_rows:, :]
    xc = cb_ref[...] + cw_ref[LRU_CONV - 1:LRU_CONV, :] * zx
    for j in range(1, LRU_CONV):
        off = hist_rows - SUBLANES * j
        xc = xc + cw_ref[LRU_CONV - 1 - j:LRU_CONV - j, :] * ext[off:off + rows, :]

    xcb = xc.astype(BF16)
    gate_r = jax.nn.sigmoid(_mm(xcb, wa_ref[...]) + ba_ref[...])
    gate_i = jax.nn.sigmoid(_mm(xcb, wx_ref[...]) + bx_ref[...])
    log_a = -LRU_C * gate_r * _softplus(-lam_ref[...])
    a = jnp.exp(log_a)
    a_ref[...] = a
    b_ref[...] = jnp.sqrt(-jnp.tanh(log_a) * (1.0 + a * a)) * gate_i * xc

    def step(t, h):
        r0 = pl.multiple_of(t * SUBLANES, SUBLANES)
        h = a_ref[pl.ds(r0, SUBLANES), :] * h + b_ref[pl.ds(r0, SUBLANES), :]
        b_ref[pl.ds(r0, SUBLANES), :] = h
        return h

    h_ref[...] = lax.fori_loop(0, rows // SUBLANES, step, h_ref[...], unroll=8)
    o_ref[...] = (b_ref[...] * jax.nn.gelu(zy)).astype(o_ref.dtype)


def _lru_branch(x, w_lx, w_ly, conv_w, conv_b, wa_bd, ba, wx_bd, bx, lam, l, rows):
    t, dm = x.shape
    wdt = w_lx.shape[2]
    params = (w_lx, w_ly, conv_w, conv_b, wa_bd, ba, wx_bd, bx, lam)
    return pl.pallas_call(
        _lru_kernel,
        grid=(t // rows,),
        in_specs=[pl.BlockSpec((rows, dm), lambda i: (i, 0))] + [_layer_spec(p, l) for p in params],
        out_specs=pl.BlockSpec((rows, wdt), lambda i: (i, 0)),
        out_shape=jax.ShapeDtypeStruct((t, wdt), BF16),
        scratch_shapes=[pltpu.VMEM(((LRU_CONV - 1) * SUBLANES, wdt), F32),
                        pltpu.VMEM((SUBLANES, wdt), F32),
                        pltpu.VMEM((rows, wdt), F32),
                        pltpu.VMEM((rows, wdt), F32)],
        compiler_params=_cparams("arbitrary"),
        name="lru_branch",
    )(x, *params)


def _swap_tile_and_lane_group(tiles, grp):
    n = len(tiles)
    lane_g = lax.broadcasted_iota(jnp.int32, tiles[0].shape, 1) // grp
    k = 1
    while k < n:
        high = (lane_g & k) != 0
        nxt = list(tiles)
        for p in range(n):
            if p & k == 0:
                lo_t, hi_t = tiles[p], tiles[p + k]
                nxt[p] = jnp.where(high, pltpu.roll(hi_t, k * grp, 1), lo_t)
                nxt[p + k] = jnp.where(high, hi_t, pltpu.roll(lo_t, LANES - k * grp, 1))
        tiles = nxt
        k *= 2
    return tiles


def _rw_prep_kernel(x_ref, w_ref, mu_ref, w0_ref, wup_ref, a0_ref, aup_ref, gup_ref,
                    kk_ref, ka_ref, rk_ref, ones_ref, sel_b_ref, sel_k_ref,
                    k5_ref, vs_ref, g_ref, bonus_ref, prev_ref):
    tt = x_ref.shape[0]
    cols = w_ref.shape[1]
    wdt = w0_ref.shape[1]
    lora0 = 5 * wdt
    lora1 = lora0 + wup_ref.shape[0]
    b = pl.program_id(1)

    @pl.when(pl.program_id(0) == 0)
    def _():
        prev_ref[b] = jnp.zeros(prev_ref.shape[1:], F32)

    z = _mm(x_ref[...].astype(BF16), w_ref[...])
    first_row = lax.broadcasted_iota(jnp.int32, (tt, cols), 0) == 0
    shifted = jnp.where(first_row, prev_ref[b, 0:1, :], pltpu.roll(z, 1, 0))
    prev_ref[b, 0:1, :] = z[tt - 1:tt, :]
    z = z + mu_ref[...] * (shifted - z)

    r = z[:, 0:wdt]
    k = z[:, wdt:2 * wdt]
    v_rep = z[:, 2 * wdt:4 * wdt]
    v = z[:, 4 * wdt:5 * wdt]
    lora = z[:, lora0:lora1]
    gd = z[:, lora1:]

    w_log = -_softplus(-(w0_ref[...] + _mm(jnp.tanh(lora).astype(BF16), wup_ref[...]))) - 0.5
    decay = jnp.exp(-jnp.exp(w_log))
    a = jax.nn.sigmoid(a0_ref[...] + _mm(lora.astype(BF16), aup_ref[...]))
    g_ref[...] = _mm(jax.nn.sigmoid(gd).astype(BF16), gup_ref[...])

    ones_h = ones_ref[...]
    kk = k * kk_ref[...]
    kk = kk * lax.rsqrt(_segsum(kk * kk, ones_h) + 1e-12)
    k = k * (1.0 + (a - 1.0) * ka_ref[...])
    kka = kk * a

    k5_ref[:, 0:wdt] = -kk
    k5_ref[:, wdt:2 * wdt] = decay
    k5_ref[:, 2 * wdt:3 * wdt] = kka
    k5_ref[:, 3 * wdt:4 * wdt] = k
    k5_ref[:, 4 * wdt:5 * wdt] = decay * r
    vs_ref[:, 0:2 * wdt] = v_rep
    vs_ref[:, 2 * wdt:2 * wdt + LANES] = _segsum(kka * r, sel_b_ref[...]) + _segsum(k * r, sel_k_ref[...])
    bonus_ref[...] = _segsum(r * k * rk_ref[...], ones_h) * v


def _rw_prep(x, w_rw, mu, w0, wup_pad, a0, aup_pad, gup, k_k, k_a, r_k, ones_h, sel_b, sel_k, l, bsz, tt):
    seq = x.shape[0]
    dm = x.shape[1] // bsz
    wdt = w0.shape[2]
    cols = w_rw.shape[2]
    params = (w_rw, mu, w0, wup_pad, a0, aup_pad, gup, k_k, k_a, r_k)
    per_batch = lambda n: _seq_spec(tt, n)
    widths = (5 * wdt, 2 * wdt + LANES, wdt, wdt)
    return pl.pallas_call(
        _rw_prep_kernel,
        grid=(seq // tt, bsz),
        in_specs=[per_batch(dm)] + [_layer_spec(p, l) for p in params]
                 + [_const_spec(ones_h.shape), _const_spec(sel_b.shape), _const_spec(sel_k.shape)],
        out_specs=[per_batch(n) for n in widths],
        out_shape=[jax.ShapeDtypeStruct((seq, bsz * n), F32) for n in widths],
        scratch_shapes=[pltpu.VMEM((bsz, SUBLANES, cols), F32)],
        compiler_params=_cparams("arbitrary", "arbitrary"),
        name="rw_prep",
    )(x, *params, ones_h, sel_b, sel_k)


def _rw_scan_kernel(k5_ref, vs_ref, y_ref, st_ref, k5p_ref, v2x_ref, ypl_ref, *, bsz, heads):
    tt = k5_ref.shape[0]
    kp_n = st_ref.shape[0]
    hd = st_ref.shape[1]
    wdt = hd * heads
    grp = LANES // bsz
    k5w = k5_ref.shape[1] // bsz
    vsw = vs_ref.shape[1] // bsz
    vt = hd // SUBLANES
    @pl.when(pl.program_id(0) == 0)
    def _():
        st_ref[...] = jnp.zeros_like(st_ref)

    rc = min(tt, RELAYOUT_ROWS)

    def batch_tiles(src_ref, width, blk, r0):
        return [src_ref[r0:r0 + rc, pl.ds(pl.multiple_of(b * width + blk * LANES, LANES), LANES)]
                for b in range(bsz)]

    def k5_block(blk, carry):
        for r0 in range(0, tt, rc):
            for i, tile in enumerate(_swap_tile_and_lane_group(batch_tiles(k5_ref, k5w, blk, r0), grp)):
                k5p_ref[blk * bsz + i, r0:r0 + rc, :] = tile
        return carry

    def vs_block(blk, carry):
        for r0 in range(0, tt, rc):
            for i, tile in enumerate(_swap_tile_and_lane_group(batch_tiles(vs_ref, vsw, blk, r0), grp)):
                v2x_ref[blk, r0:r0 + rc, i, :] = tile
        return carry

    lax.fori_loop(0, k5w // LANES, k5_block, 0)
    lax.fori_loop(0, vsw // LANES, vs_block, 0)

    low_half = (lax.broadcasted_iota(jnp.int32, (hd, LANES), 1) // heads) % 2 == 0

    def fold(p):
        return p + jnp.where(low_half, pltpu.roll(p, LANES - heads, 1), pltpu.roll(p, heads, 1))

    def row(j, kp, t):
        return k5p_ref[j * kp_n + kp, pl.ds(t, 1), :]

    def step(t, carry):
        vv = jnp.concatenate([v2x_ref[m, t] for m in range(vt)], axis=0)
        b_r = v2x_ref[vt, t, 0:1, :]
        k_r = v2x_ref[vt, t, 1:2, :]
        sa = jnp.zeros((hd, LANES), F32)
        yp = jnp.zeros((hd, LANES), F32)
        for kp in range(kp_n):
            s_k = st_ref[kp]
            sa = sa + s_k * row(0, kp, t)
            yp = yp + s_k * row(4, kp, t)
        sa = fold(sa)
        yp = fold(yp)
        for kp in range(kp_n):
            st_ref[kp] = st_ref[kp] * row(1, kp, t) + sa * row(2, kp, t) + vv * row(3, kp, t)
        y = yp + sa * b_r + vv * k_r
        for v in range(hd):
            ypl_ref[v, pl.ds(t, 1), :] = y[v:v + 1, :]
        return carry

    lax.fori_loop(0, tt, step, 0)

    pair_lo = lax.broadcasted_iota(jnp.int32, (rc, LANES), 1) % grp < heads

    def out_block(vb, carry):
        for r0 in range(0, tt, rc):
            pairs = []
            for m in range(bsz):
                even = ypl_ref[vb * 2 * bsz + 2 * m, r0:r0 + rc, :]
                odd = ypl_ref[vb * 2 * bsz + 2 * m + 1, r0:r0 + rc, :]
                pairs.append(jnp.where(pair_lo, even, pltpu.roll(odd, heads, 1)))
            for b, tile in enumerate(_swap_tile_and_lane_group(pairs, grp)):
                y_ref[r0:r0 + rc, pl.ds(pl.multiple_of(b * wdt + vb * LANES, LANES), LANES)] = tile
        return carry

    lax.fori_loop(0, wdt // LANES, out_block, 0)


def _rw_scan(k5, vs, bsz, heads, tt):
    seq = k5.shape[0]
    hd = HEAD_DIM
    wdt = hd * heads
    n_k5 = k5.shape[1] // LANES
    n_vs = vs.shape[1] // bsz // LANES
    return pl.pallas_call(
        functools.partial(_rw_scan_kernel, bsz=bsz, heads=heads),
        grid=(seq // tt,),
        in_specs=[pl.BlockSpec((tt, k5.shape[1]), lambda i: (i, 0)),
                  pl.BlockSpec((tt, vs.shape[1]), lambda i: (i, 0))],
        out_specs=pl.BlockSpec((tt, bsz * wdt), lambda i: (i, 0)),
        out_shape=jax.ShapeDtypeStruct((seq, bsz * wdt), F32),
        scratch_shapes=[pltpu.VMEM((n_k5 // 5, hd, LANES), F32),
                        pltpu.VMEM((n_k5, tt, LANES), F32),
                        pltpu.VMEM((n_vs, tt, bsz, LANES), F32),
                        pltpu.VMEM((hd, tt, LANES), F32)],
        compiler_params=_cparams("arbitrary"),
        name="rw_scan",
    )(k5, vs)


def _merge_kernel(x_ref, ys5_ref, ylru_ref, yrw_ref, bonus_ref, g_ref,
                  wgate_ref, glu1_ref, glu2_ref, lruo_ref, rwo_ref, mix_ref,
                  gng_ref, gnb_ref, ones_ref, lng_ref, lnb_ref, o_ref, *, alpha):
    x = x_ref[...]
    dm = x.shape[1]
    gates = jax.nn.sigmoid(_mm(x.astype(BF16), wgate_ref[...]))

    ys5 = ys5_ref[...].astype(BF16)
    y_s5 = _mm(ys5, glu1_ref[...]) * jax.nn.sigmoid(_mm(ys5, glu2_ref[...]))
    y_lru = _mm(ylru_ref[...].astype(BF16), lruo_ref[...])

    ones_bd = ones_ref[...]
    inv_n = 1.0 / HEAD_DIM
    y = yrw_ref[...]
    yc = y - _segsum(y, ones_bd) * inv_n
    var = _segsum(yc * yc, ones_bd) * inv_n
    y = yc * lax.rsqrt(var + RW_GN_EPS) * gng_ref[...] + gnb_ref[...]
    y = (y + bonus_ref[...]) * g_ref[...]
    y_rw = _mm(y.astype(BF16), rwo_ref[...])

    merged = (gates[:, 0:dm] * y_s5 + gates[:, dm:2 * dm] * y_lru + gates[:, 2 * dm:3 * dm] * y_rw)
    o_ref[...] = _layer_norm(alpha * x + _mm(merged.astype(BF16), mix_ref[...]),
                             lng_ref[...], lnb_ref[...])


def _merge(x, ys5, ylru, yrw, bonus, g, w_gate, glu1, glu2, lru_o, rw_o, mix, gn_g, gn_b, ones_bd,
           ln_g, ln_b, l, alpha, tm, bsz):
    seq = x.shape[0]
    dm = x.shape[1] // bsz
    wdt = ys5.shape[1] // bsz
    params = (w_gate, glu1, glu2, lru_o, rw_o, mix, gn_g, gn_b)
    return pl.pallas_call(
        functools.partial(_merge_kernel, alpha=alpha),
        grid=(seq // tm, bsz),
        in_specs=[_seq_spec(tm, dm)] + [_seq_spec(tm, wdt)] * 5 + [_layer_spec(p, l) for p in params]
                 + [_const_spec(ones_bd.shape), _layer_spec(ln_g, l), _layer_spec(ln_b, l)],
        out_specs=_seq_spec(tm, dm),
        out_shape=jax.ShapeDtypeStruct((seq, bsz * dm), F32),
        compiler_params=_cparams("parallel", "parallel"),
        name="merge_mix_ln",
    )(x, ys5, ylru, yrw, bonus, g, *params, ones_bd, ln_g, ln_b)


def _kv_kernel(mem_ref, w_ref, o_ref):
    o_ref[...] = _mm(mem_ref[...].astype(BF16), w_ref[...]).astype(BF16)


def _kv_proj(mem2, wkv, tn):
    rows, dm = mem2.shape
    depth, _, n = wkv.shape
    return pl.pallas_call(
        _kv_kernel,
        grid=(depth, n // tn),
        in_specs=[pl.BlockSpec((rows, dm), lambda l, j: (0, 0)),
                  pl.BlockSpec((None, dm, tn), lambda l, j: (l, 0, j))],
        out_specs=pl.BlockSpec((None, rows, tn), lambda l, j: (l, 0, j)),
        out_shape=jax.ShapeDtypeStruct((depth, rows, n), BF16),
        compiler_params=_cparams("parallel", "parallel"),
        name="xa_kv_proj",
    )(mem2, wkv)


def _xattn_kernel(x_ref, k_ref, v_ref, wq_ref, wo_ref, g_ref, b_ref, o_ref, *, alpha, heads):
    x = x_ref[...]
    dm = x.shape[1]
    hd = dm // heads
    q = _mm(x.astype(BF16), wq_ref[...]).astype(BF16)
    outs = []
    for h in range(heads):
        sl = slice(h * hd, (h + 1) * hd)
        s = lax.dot_general(q[:, sl], k_ref[:, sl], (((1,), (1,)), ((), ())),
                            preferred_element_type=F32) * (hd ** -0.5)
        e = jnp.exp(s - jnp.max(s, -1, keepdims=True))
        p = e / jnp.sum(e, -1, keepdims=True)
        outs.append(_mm(p.astype(BF16), v_ref[:, sl]))
    o = jnp.concatenate(outs, axis=1)
    o_ref[...] = _layer_norm(alpha * x + _mm(o.astype(BF16), wo_ref[...]), g_ref[...], b_ref[...])


def _xattn(x, kv, wq, wo, g, b, l, alpha, bsz, ts):
    s = x.shape[0]
    dm = x.shape[1] // bsz
    m = kv.shape[1] // bsz
    seq = pl.BlockSpec((ts, dm), lambda bi, i: (i, bi))
    return pl.pallas_call(
        functools.partial(_xattn_kernel, alpha=alpha, heads=XA_HEADS),
        grid=(bsz, s // ts),
        in_specs=[seq,
                  pl.BlockSpec((None, m, dm), lambda bi, i: (l, bi, 0)),
                  pl.BlockSpec((None, m, dm), lambda bi, i: (l, bi, 1)),
                  _layer_spec(wq, l), _layer_spec(wo, l), _layer_spec(g, l), _layer_spec(b, l)],
        out_specs=seq,
        out_shape=jax.ShapeDtypeStruct((s, bsz * dm), F32),
        compiler_params=_cparams("parallel", "parallel"),
        name="xattn_ln",
    )(x, kv, kv, wq, wo, g, b)


def _block_diag(w):
    depth, h, n, _ = w.shape
    eye = jnp.eye(h, dtype=w.dtype)
    return (w[:, :, :, None, :] * eye[None, :, None, :, None]).reshape(depth, h * n, h * n)


def _s5_discretise(lam_re, lam_im, log_dt, b_re, b_im, c_re, c_im):
    depth, g, p = lam_re.shape
    c = b_re.shape[-1]
    gb = LANES // c
    nq = g // gb
    dt = jnp.exp(log_dt)[..., None]
    mag = jnp.exp(lam_re * dt)
    a_re = mag * jnp.cos(lam_im * dt)
    a_im = mag * jnp.sin(lam_im * dt)
    den = lam_re * lam_re + lam_im * lam_im
    co_re = ((a_re - 1.0) * lam_re + a_im * lam_im) / den
    co_im = (a_im * lam_re - (a_re - 1.0) * lam_im) / den
    bb_re = co_re[..., None] * b_re - co_im[..., None] * b_im
    bb_im = co_re[..., None] * b_im + co_im[..., None] * b_re
    eye = jnp.eye(gb, dtype=F32)

    def b_layout(w):
        w = jnp.swapaxes(w.reshape(depth, nq, gb, p, c), 3, 4)
        return (w[:, :, :, :, None, :] * eye[None, None, :, None, :, None]).reshape(depth, nq, gb * c, gb * p)

    def c_layout(w):
        w = jnp.swapaxes(w.reshape(depth, nq, gb, c, p), 3, 4)
        return (w[:, :, :, :, None, :] * eye[None, None, :, None, :, None]).reshape(depth, nq, gb * p, gb * c)

    bq = jnp.concatenate([b_layout(bb_re), b_layout(bb_im)], axis=-1).astype(BF16)
    cq = jnp.concatenate([c_layout(c_re), c_layout(-c_im)], axis=-2).astype(BF16)
    return bq, cq, a_re.reshape(depth, 1, g * p), a_im.reshape(depth, 1, g * p)


def _tile(n, target):
    return min(n, target)


def kernel(x, mem, ffn1_wg, ffn1_wu, ffn1_wd, ln1_g, ln1_b, w_in, s5_lam_re, s5_lam_im, s5_log_dt, s5_b_re, s5_b_im, s5_c_re, s5_c_im, s5_d, s5_glu_w1, s5_glu_w2, lru_conv_w, lru_conv_b, lru_wa, lru_ba, lru_wx, lru_bx, lru_lambda, lru_w_out, rw_mu, rw_w0, rw_w_up, rw_a0, rw_a_up, rw_g_up, rw_k_k, rw_k_a, rw_r_k, rw_ln_g, rw_ln_b, rw_w_out, mix_w_out, ln2_g, ln2_b, xa_wq, xa_wkv, xa_wo, ln3_g, ln3_b, ffn2_wg, ffn2_wu, ffn2_wd, ln4_g, ln4_b):
    bsz, seq, dm = x.shape
    depth = w_in.shape[0]
    wdt = s5_d.shape[1]
    heads = wdt // HEAD_DIM
    alpha = (2 * depth) ** 0.25
    assert bsz == SUBLANES, "one time step must be one aligned 8-row group"
    t = bsz * seq

    bf = lambda w: w.astype(BF16)
    vec = lambda p: p.reshape(depth, 1, -1)

    w_in_b = bf(w_in)
    w_s5, w_lx, w_ly = w_in_b[:, :, 0:wdt], w_in_b[:, :, wdt:2 * wdt], w_in_b[:, :, 2 * wdt:3 * wdt]
    n_rw = rw_mu.shape[1]
    w_rw = w_in_b[:, :, 3 * wdt:3 * wdt + n_rw]
    w_gate = w_in_b[:, :, 3 * wdt + n_rw:]
    ffn1 = (bf(ffn1_wg), bf(ffn1_wu), bf(ffn1_wd), vec(ln1_g), vec(ln1_b))
    ffn2 = (bf(ffn2_wg), bf(ffn2_wu), bf(ffn2_wd), vec(ln4_g), vec(ln4_b))

    bq, cq, a_re, a_im = _s5_discretise(s5_lam_re, s5_lam_im, s5_log_dt, s5_b_re, s5_b_im, s5_c_re, s5_c_im)
    wa_bd, wx_bd = bf(_block_diag(lru_wa)), bf(_block_diag(lru_wx))

    kh_n = LANES // (bsz * heads)
    assert kh_n == 2 and HEAD_DIM % kh_n == 0, "lane layout b*16 + kh*8 + h needs batch * heads * 2 == 128"
    kp_n = HEAD_DIM // kh_n
    key_perm = np.array([h_ * HEAD_DIM + kh_ * kp_n + kp_
                         for kp_ in range(kp_n) for kh_ in range(kh_n) for h_ in range(heads)])
    val_perm = np.array([h_ * HEAD_DIM + v_ for v_ in range(HEAD_DIM) for h_ in range(heads)])
    val_rep_perm = np.array([h_ * HEAD_DIM + v_
                             for v_ in range(HEAD_DIM) for _ in range(kh_n) for h_ in range(heads)])

    def rw_layout(p):
        return jnp.concatenate([p[..., 0:wdt][..., key_perm], p[..., wdt:2 * wdt][..., key_perm],
                                p[..., 2 * wdt:3 * wdt][..., val_rep_perm], p[..., 2 * wdt:3 * wdt][..., val_perm],
                                p[..., 3 * wdt:]], axis=-1)

    w_rw_p = rw_layout(w_rw)
    mu_p = vec(rw_layout(rw_mu))
    zeros_dr = jnp.zeros((depth, RW_A_RANK, wdt), BF16)
    zeros_ar = jnp.zeros((depth, RW_DECAY_RANK, wdt), BF16)
    wup_pad = jnp.concatenate([bf(rw_w_up)[:, :, key_perm], zeros_dr], axis=1)
    aup_pad = jnp.concatenate([zeros_ar, bf(rw_a_up)[:, :, key_perm]], axis=1)
    gup_p = bf(rw_g_up)[:, :, val_perm]
    w0_p, a0_p = vec(rw_w0[:, key_perm]), vec(rw_a0[:, key_perm])
    kk_p, ka_p = vec(rw_k_k[:, key_perm]), vec(rw_k_a[:, key_perm])
    rk_p = vec(rw_r_k.reshape(depth, wdt)[:, key_perm])
    gn_g_p, gn_b_p = vec(rw_ln_g[:, val_perm]), vec(rw_ln_b[:, val_perm])
    rw_o = bf(rw_w_out)[:, val_perm, :]
    head_of = jnp.arange(wdt) % heads
    ones_h = (head_of[:, None] == head_of[None, :]).astype(BF16)
    lane = jnp.arange(LANES)
    grp = kh_n * heads
    sel_b = ((lane[None, :] < grp) & (lane[None, :] % heads == head_of[:, None])).astype(BF16)
    sel_k = ((lane[None, :] >= grp) & (lane[None, :] < 2 * grp)
             & (lane[None, :] % heads == head_of[:, None])).astype(BF16)

    kv = _kv_proj(mem.reshape(-1, dm), bf(xa_wkv), 512)
    wq_b, wo_b = bf(xa_wq), bf(xa_wo)
    glu1, glu2, lru_o, mix = bf(s5_glu_w1), bf(s5_glu_w2), bf(lru_w_out), bf(mix_w_out)

    tm = _tile(seq, 512)

    h = x
    for l in range(depth):
        h, hb = _ffn_ln(h, *ffn1, l, alpha, tm, bsz, batch_major_in=(l == 0), with_bf16_copy=True)

        h_rows = hb.reshape(t, dm)
        y_s5 = _s5_branch(h_rows, w_s5, bq, cq, a_re, a_im, vec(s5_d), l, _tile(t, 512))
        y_lru = _lru_branch(h_rows, w_lx, w_ly, lru_conv_w, vec(lru_conv_b), wa_bd, vec(lru_ba), wx_bd,
                            vec(lru_bx), vec(lru_lambda), l, _tile(t, 1024))
        k5, vs, g, bonus = _rw_prep(hb, w_rw_p, mu_p, w0_p, wup_pad, a0_p, aup_pad, gup_p, kk_p, ka_p, rk_p,
                                    ones_h, sel_b, sel_k, l, bsz, _tile(seq, 512))
        y_rw = _rw_scan(k5, vs, bsz, heads, _tile(seq, 64))

        h = _merge(h, y_s5.reshape(seq, bsz * wdt), y_lru.reshape(seq, bsz * wdt), y_rw, bonus, g,
                   w_gate, glu1, glu2, lru_o, rw_o, mix,
                   gn_g_p, gn_b_p, ones_h, vec(ln2_g), vec(ln2_b), l, alpha, tm, bsz)
        h = _xattn(h, kv, wq_b, wo_b, vec(ln3_g), vec(ln3_b), l, alpha, bsz, _tile(seq, 512))
        h = _ffn_ln(h, *ffn2, l, alpha, tm, bsz, batch_major_out=(l == depth - 1))
    return h
```

```python
import functools
import math

import jax
import jax.numpy as jnp
import numpy as np
from jax import lax
from jax.experimental import pallas as pl
from jax.experimental.pallas import tpu as pltpu

F32 = jnp.float32
BF16 = jnp.bfloat16

LN_EPS = 1e-5
RW_GN_EPS = 64e-5
LRU_C = 8.0
LRU_CONV = 4
S5_GROUP = 16
S5_STATE = 64
HEAD_DIM = 64
XA_HEADS = 4
RW_DECAY_RANK = 64
RW_A_RANK = 64
RW_GATE_RANK = 128

SUBLANES = 8
LANES = 128
VMEM_LIMIT_BYTES = 56 * 1024 * 1024
RELAYOUT_ROWS = 64


def _cparams(*sem):
    return pltpu.CompilerParams(dimension_semantics=sem, vmem_limit_bytes=VMEM_LIMIT_BYTES)


def _const_spec(shape):
    nd = len(shape)
    return pl.BlockSpec(shape, lambda *_: (0,) * nd, pipeline_mode=pl.Buffered(1))


def _layer_spec(arr, l):
    tail = arr.shape[1:]
    nd = len(tail)
    return pl.BlockSpec((None,) + tail, lambda *_: (l,) + (0,) * nd, pipeline_mode=pl.Buffered(1))


def _layer_norm(y, g, b):
    mu = jnp.mean(y, -1, keepdims=True)
    yc = y - mu
    var = jnp.mean(yc * yc, -1, keepdims=True)
    return yc * lax.rsqrt(var + LN_EPS) * g + b


def _mm(a, b):
    return jnp.dot(a, b, preferred_element_type=F32)


def _segsum(x, ones_bd):
    hi = x.astype(BF16)
    lo = (x - hi.astype(F32)).astype(BF16)
    return _mm(hi, ones_bd) + _mm(lo, ones_bd)


def _softplus(x):
    return jnp.maximum(x, 0.0) + jnp.log1p(jnp.exp(-jnp.abs(x)))


def _ffn_kernel(x_ref, wg_ref, wu_ref, wd_ref, g_ref, b_ref, o_ref, *maybe_ob_ref, alpha, f_chunk):
    x = x_ref[...]
    xb = x.astype(BF16)
    d_ff = wg_ref.shape[1]
    acc = jnp.zeros(x.shape, F32)
    for c0 in range(0, d_ff, f_chunk):
        hg = _mm(xb, wg_ref[:, c0:c0 + f_chunk])
        hu = _mm(xb, wu_ref[:, c0:c0 + f_chunk])
        h = hg * jax.nn.sigmoid(hg) * hu
        acc = acc + _mm(h.astype(BF16), wd_ref[c0:c0 + f_chunk, :])
    out = _layer_norm(alpha * x + 0.5 * acc, g_ref[...], b_ref[...])
    o_ref[...] = out
    for ob_ref in maybe_ob_ref:
        ob_ref[...] = out.astype(ob_ref.dtype)


def _seq_spec(tm, n):
    return pl.BlockSpec((tm, n), lambda i, b: (i, b))


def _batch_major_spec(tm, n):
    return pl.BlockSpec((None, tm, n), lambda i, b: (b, i, 0))


def _ffn_ln(x, wg, wu, wd, g, b, l, alpha, tm, bsz, batch_major_in=False, batch_major_out=False,
            with_bf16_copy=False):
    d = wg.shape[1]
    seq = x.shape[1] if batch_major_in else x.shape[0]
    out_shape = (bsz, seq, d) if batch_major_out else (seq, bsz * d)
    out_spec = _batch_major_spec(tm, d) if batch_major_out else _seq_spec(tm, d)
    out_specs, out_shapes = out_spec, jax.ShapeDtypeStruct(out_shape, F32)
    if with_bf16_copy:
        out_specs = [out_spec, out_spec]
        out_shapes = [out_shapes, jax.ShapeDtypeStruct(out_shape, BF16)]
    return pl.pallas_call(
        functools.partial(_ffn_kernel, alpha=alpha, f_chunk=256),
        grid=(seq // tm, bsz),
        in_specs=[_batch_major_spec(tm, d) if batch_major_in else _seq_spec(tm, d),
                  _layer_spec(wg, l), _layer_spec(wu, l), _layer_spec(wd, l),
                  _layer_spec(g, l), _layer_spec(b, l)],
        out_specs=out_specs,
        out_shape=out_shapes,
        compiler_params=_cparams("parallel", "parallel"),
        name="ffn_ln",
    )(x, wg, wu, wd, g, b)


def _s5_kernel(x_ref, w_ref, bq_ref, cq_ref, are_ref, aim_ref, d_ref, o_ref,
               st_ref, carry_ref, u_ref):
    rows = x_ref.shape[0]
    nq = bq_ref.shape[0]
    half = bq_ref.shape[2] // 2
    cw = bq_ref.shape[1]

    @pl.when(pl.program_id(0) == 0)
    def _():
        carry_ref[...] = jnp.zeros_like(carry_ref)

    u = _mm(x_ref[...].astype(BF16), w_ref[...])
    u_ref[...] = u
    ub = u.astype(BF16)
    for q in range(nq):
        st_ref[:, 2 * half * q:2 * half * (q + 1)] = _mm(ub[:, cw * q:cw * (q + 1)], bq_ref[q])

    def step(t, s):
        r0 = pl.multiple_of(t * SUBLANES, SUBLANES)
        parts = []
        for q in range(nq):
            re = s[:, 2 * half * q:2 * half * q + half]
            im = s[:, 2 * half * q + half:2 * half * (q + 1)]
            ar = are_ref[:, half * q:half * (q + 1)]
            ai = aim_ref[:, half * q:half * (q + 1)]
            bre = st_ref[pl.ds(r0, SUBLANES), 2 * half * q:2 * half * q + half]
            bim = st_ref[pl.ds(r0, SUBLANES), 2 * half * q + half:2 * half * (q + 1)]
            parts.append(ar * re - ai * im + bre)
            parts.append(ar * im + ai * re + bim)
        s_new = jnp.concatenate(parts, axis=1)
        st_ref[pl.ds(r0, SUBLANES), :] = s_new
        return s_new

    carry_ref[...] = lax.fori_loop(0, rows // SUBLANES, step, carry_ref[...])

    for q in range(nq):
        y = _mm(st_ref[:, 2 * half * q:2 * half * (q + 1)].astype(BF16), cq_ref[q])
        y = y + d_ref[:, cw * q:cw * (q + 1)] * u_ref[:, cw * q:cw * (q + 1)]
        o_ref[:, cw * q:cw * (q + 1)] = jax.nn.gelu(y).astype(o_ref.dtype)


def _s5_branch(x, w_s5, bq, cq, a_re, a_im, d, l, rows):
    t, dm = x.shape
    wdt = w_s5.shape[2]
    n_state = a_re.shape[2]
    return pl.pallas_call(
        _s5_kernel,
        grid=(t // rows,),
        in_specs=[pl.BlockSpec((rows, dm), lambda i: (i, 0)),
                  _layer_spec(w_s5, l), _layer_spec(bq, l), _layer_spec(cq, l),
                  _layer_spec(a_re, l), _layer_spec(a_im, l), _layer_spec(d, l)],
        out_specs=pl.BlockSpec((rows, wdt), lambda i: (i, 0)),
        out_shape=jax.ShapeDtypeStruct((t, wdt), BF16),
        scratch_shapes=[pltpu.VMEM((rows, 2 * n_state), F32),
                        pltpu.VMEM((SUBLANES, 2 * n_state), F32),
                        pltpu.VMEM((rows, wdt), F32)],
        compiler_params=_cparams("arbitrary"),
        name="s5_branch",
    )(x, w_s5, bq, cq, a_re, a_im, d)


def _lru_kernel(x_ref, wlx_ref, wly_ref, cw_ref, cb_ref, wa_ref, ba_ref, wx_ref, bx_ref,
                lam_ref, o_ref, hist_ref, h_ref, a_ref, b_ref):
    rows = x_ref.shape[0]
    hist_rows = hist_ref.shape[0]

    @pl.when(pl.program_id(0) == 0)
    def _():
        hist_ref[...] = jnp.zeros_like(hist_ref)
        h_ref[...] = jnp.zeros_like(h_ref)

    xb = x_ref[...].astype(BF16)
    zx = _mm(xb, wlx_ref[...])
    zy = _mm(xb, wly_ref[...])

    ext = jnp.concatenate([hist_ref[...], zx], axis=0)
    hist_ref[...] = zx[rows - hist_rows:, :]
    xc = cb_ref[...] + cw_ref[LRU_CONV - 1:LRU_CONV, :] * zx
    for j in range(1, LRU_CONV):
        off = hist_rows - SUBLANES * j
        xc = xc + cw_ref[LRU_CONV - 1 - j:LRU_CONV - j, :] * ext[off:off + rows, :]

    xcb = xc.astype(BF16)
    gate_r = jax.nn.sigmoid(_mm(xcb, wa_ref[...]) + ba_ref[...])
    gate_i = jax.nn.sigmoid(_mm(xcb, wx_ref[...]) + bx_ref[...])
    log_a = -LRU_C * gate_r * _softplus(-lam_ref[...])
    a = jnp.exp(log_a)
    a_ref[...] = a
    b_ref[...] = jnp.sqrt(-jnp.tanh(log_a) * (1.0 + a * a)) * gate_i * xc

    def step(t, h):
        r0 = pl.multiple_of(t * SUBLANES, SUBLANES)
        h = a_ref[pl.ds(r0, SUBLANES), :] * h + b_ref[pl.ds(r0, SUBLANES), :]
        b_ref[pl.ds(r0, SUBLANES), :] = h
        return h

    h_ref[...] = lax.fori_loop(0, rows // SUBLANES, step, h_ref[...], unroll=8)
    o_ref[...] = (b_ref[...] * jax.nn.gelu(zy)).astype(o_ref.dtype)


def _lru_branch(x, w_lx, w_ly, conv_w, conv_b, wa_bd, ba, wx_bd, bx, lam, l, rows):
    t, dm = x.shape
    wdt = w_lx.shape[2]
    params = (w_lx, w_ly, conv_w, conv_b, wa_bd, ba, wx_bd, bx, lam)
    return pl.pallas_call(
        _lru_kernel,
        grid=(t // rows,),
        in_specs=[pl.BlockSpec((rows, dm), lambda i: (i, 0))] + [_layer_spec(p, l) for p in params],
        out_specs=pl.BlockSpec((rows, wdt), lambda i: (i, 0)),
        out_shape=jax.ShapeDtypeStruct((t, wdt), BF16),
        scratch_shapes=[pltpu.VMEM(((LRU_CONV - 1) * SUBLANES, wdt), F32),
                        pltpu.VMEM((SUBLANES, wdt), F32),
                        pltpu.VMEM((rows, wdt), F32),
                        pltpu.VMEM((rows, wdt), F32)],
        compiler_params=_cparams("arbitrary"),
        name="lru_branch",
    )(x, *params)


def _swap_tile_and_lane_group(tiles, grp):
    n = len(tiles)
    lane_g = lax.broadcasted_iota(jnp.int32, tiles[0].shape, 1) // grp
    k = 1
    while k < n:
        high = (lane_g & k) != 0
        nxt = list(tiles)
        for p in range(n):
            if p & k == 0:
                lo_t, hi_t = tiles[p], tiles[p + k]
                nxt[p] = jnp.where(high, pltpu.roll(hi_t, k * grp, 1), lo_t)
                nxt[p + k] = jnp.where(high, hi_t, pltpu.roll(lo_t, LANES - k * grp, 1))
        tiles = nxt
        k *= 2
    return tiles


def _rw_prep_kernel(x_ref, w_ref, mu_ref, w0_ref, wup_ref, a0_ref, aup_ref, gup_ref,
                    kk_ref, ka_ref, rk_ref, ones_ref, sel_b_ref, sel_k_ref,
                    k5_ref, vs_ref, g_ref, bonus_ref, prev_ref):
    tt = x_ref.shape[0]
    cols = w_ref.shape[1]
    wdt = w0_ref.shape[1]
    lora0 = 5 * wdt
    lora1 = lora0 + wup_ref.shape[0]
    b = pl.program_id(1)

    @pl.when(pl.program_id(0) == 0)
    def _():
        prev_ref[b] = jnp.zeros(prev_ref.shape[1:], F32)

    z = _mm(x_ref[...].astype(BF16), w_ref[...])
    first_row = lax.broadcasted_iota(jnp.int32, (tt, cols), 0) == 0
    shifted = jnp.where(first_row, prev_ref[b, 0:1, :], pltpu.roll(z, 1, 0))
    prev_ref[b, 0:1, :] = z[tt - 1:tt, :]
    z = z + mu_ref[...] * (shifted - z)

    r = z[:, 0:wdt]
    k = z[:, wdt:2 * wdt]
    v_rep = z[:, 2 * wdt:4 * wdt]
    v = z[:, 4 * wdt:5 * wdt]
    lora = z[:, lora0:lora1]
    gd = z[:, lora1:]

    w_log = -_softplus(-(w0_ref[...] + _mm(jnp.tanh(lora).astype(BF16), wup_ref[...]))) - 0.5
    decay = jnp.exp(-jnp.exp(w_log))
    a = jax.nn.sigmoid(a0_ref[...] + _mm(lora.astype(BF16), aup_ref[...]))
    g_ref[...] = _mm(jax.nn.sigmoid(gd).astype(BF16), gup_ref[...])

    ones_h = ones_ref[...]
    kk = k * kk_ref[...]
    kk = kk * lax.rsqrt(_segsum(kk * kk, ones_h) + 1e-12)
    k = k * (1.0 + (a - 1.0) * ka_ref[...])
    kka = kk * a

    k5_ref[:, 0:wdt] = -kk
    k5_ref[:, wdt:2 * wdt] = decay
    k5_ref[:, 2 * wdt:3 * wdt] = kka
    k5_ref[:, 3 * wdt:4 * wdt] = k
    k5_ref[:, 4 * wdt:5 * wdt] = decay * r
    vs_ref[:, 0:2 * wdt] = v_rep
    vs_ref[:, 2 * wdt:2 * wdt + LANES] = _segsum(kka * r, sel_b_ref[...]) + _segsum(k * r, sel_k_ref[...])
    bonus_ref[...] = _segsum(r * k * rk_ref[...], ones_h) * v


def _rw_prep(x, w_rw, mu, w0, wup_pad, a0, aup_pad, gup, k_k, k_a, r_k, ones_h, sel_b, sel_k, l, bsz, tt):
    seq = x.shape[0]
    dm = x.shape[1] // bsz
    wdt = w0.shape[2]
    cols = w_rw.shape[2]
    params = (w_rw, mu, w0, wup_pad, a0, aup_pad, gup, k_k, k_a, r_k)
    per_batch = lambda n: _seq_spec(tt, n)
    widths = (5 * wdt, 2 * wdt + LANES, wdt, wdt)
    return pl.pallas_call(
        _rw_prep_kernel,
        grid=(seq // tt, bsz),
        in_specs=[per_batch(dm)] + [_layer_spec(p, l) for p in params]
                 + [_const_spec(ones_h.shape), _const_spec(sel_b.shape), _const_spec(sel_k.shape)],
        out_specs=[per_batch(n) for n in widths],
        out_shape=[jax.ShapeDtypeStruct((seq, bsz * n), F32) for n in widths],
        scratch_shapes=[pltpu.VMEM((bsz, SUBLANES, cols), F32)],
        compiler_params=_cparams("arbitrary", "arbitrary"),
        name="rw_prep",
    )(x, *params, ones_h, sel_b, sel_k)


def _rw_scan_kernel(k5_ref, vs_ref, y_ref, st_ref, k5p_ref, v2x_ref, ypl_ref, *, bsz, heads):
    tt = k5_ref.shape[0]
    kp_n = st_ref.shape[0]
    hd = st_ref.shape[1]
    wdt = hd * heads
    grp = LANES // bsz
    k5w = k5_ref.shape[1] // bsz
    vsw = vs_ref.shape[1] // bsz
    vt = hd // SUBLANES
    @pl.when(pl.program_id(0) == 0)
    def _():
        st_ref[...] = jnp.zeros_like(st_ref)

    rc = min(tt, RELAYOUT_ROWS)

    def batch_tiles(src_ref, width, blk, r0):
        return [src_ref[r0:r0 + rc, pl.ds(pl.multiple_of(b * width + blk * LANES, LANES), LANES)]
                for b in range(bsz)]

    def k5_block(blk, carry):
        for r0 in range(0, tt, rc):
            for i, tile in enumerate(_swap_tile_and_lane_group(batch_tiles(k5_ref, k5w, blk, r0), grp)):
                k5p_ref[blk * bsz + i, r0:r0 + rc, :] = tile
        return carry

    def vs_block(blk, carry):
        for r0 in range(0, tt, rc):
            for i, tile in enumerate(_swap_tile_and_lane_group(batch_tiles(vs_ref, vsw, blk, r0), grp)):
                v2x_ref[blk, r0:r0 + rc, i, :] = tile
        return carry

    lax.fori_loop(0, k5w // LANES, k5_block, 0, unroll=4)
    lax.fori_loop(0, vsw // LANES, vs_block, 0, unroll=3)

    low_half = (lax.broadcasted_iota(jnp.int32, (hd, LANES), 1) // heads) % 2 == 0

    def fold(p):
        return p + jnp.where(low_half, pltpu.roll(p, LANES - heads, 1), pltpu.roll(p, heads, 1))

    def row(j, kp, t):
        return k5p_ref[j * kp_n + kp, pl.ds(t, 1), :]

    def step(t, carry):
        vv = jnp.concatenate([v2x_ref[m, t] for m in range(vt)], axis=0)
        b_r = v2x_ref[vt, t, 0:1, :]
        k_r = v2x_ref[vt, t, 1:2, :]
        sa = jnp.zeros((hd, LANES), F32)
        yp = jnp.zeros((hd, LANES), F32)
        for kp in range(kp_n):
            s_k = st_ref[kp]
            sa = sa + s_k * row(0, kp, t)
            yp = yp + s_k * row(4, kp, t)
        sa = fold(sa)
        yp = fold(yp)
        for kp in range(kp_n):
            st_ref[kp] = st_ref[kp] * row(1, kp, t) + sa * row(2, kp, t) + vv * row(3, kp, t)
        y = yp + sa * b_r + vv * k_r
        for v in range(hd):
            ypl_ref[v, pl.ds(t, 1), :] = y[v:v + 1, :]
        return carry

    lax.fori_loop(0, tt, step, 0)

    pair_lo = lax.broadcasted_iota(jnp.int32, (rc, LANES), 1) % grp < heads

    def out_block(vb, carry):
        for r0 in range(0, tt, rc):
            pairs = []
            for m in range(bsz):
                even = ypl_ref[vb * 2 * bsz + 2 * m, r0:r0 + rc, :]
                odd = ypl_ref[vb * 2 * bsz + 2 * m + 1, r0:r0 + rc, :]
                pairs.append(jnp.where(pair_lo, even, pltpu.roll(odd, heads, 1)))
            for b, tile in enumerate(_swap_tile_and_lane_group(pairs, grp)):
                y_ref[r0:r0 + rc, pl.ds(pl.multiple_of(b * wdt + vb * LANES, LANES), LANES)] = tile
        return carry

    lax.fori_loop(0, wdt // LANES, out_block, 0, unroll=2)


def _rw_scan(k5, vs, bsz, heads, tt):
    seq = k5.shape[0]
    hd = HEAD_DIM
    wdt = hd * heads
    n_k5 = k5.shape[1] // LANES
    n_vs = vs.shape[1] // bsz // LANES
    return pl.pallas_call(
        functools.partial(_rw_scan_kernel, bsz=bsz, heads=heads),
        grid=(seq // tt,),
        in_specs=[pl.BlockSpec((tt, k5.shape[1]), lambda i: (i, 0)),
                  pl.BlockSpec((tt, vs.shape[1]), lambda i: (i, 0))],
        out_specs=pl.BlockSpec((tt, bsz * wdt), lambda i: (i, 0)),
        out_shape=jax.ShapeDtypeStruct((seq, bsz * wdt), F32),
        scratch_shapes=[pltpu.VMEM((n_k5 // 5, hd, LANES), F32),
                        pltpu.VMEM((n_k5, tt, LANES), F32),
                        pltpu.VMEM((n_vs, tt, bsz, LANES), F32),
                        pltpu.VMEM((hd, tt, LANES), F32)],
        compiler_params=_cparams("arbitrary"),
        name="rw_scan",
    )(k5, vs)


def _merge_kernel(x_ref, ys5_ref, ylru_ref, yrw_ref, bonus_ref, g_ref,
                  wgate_ref, glu1_ref, glu2_ref, lruo_ref, rwo_ref, mix_ref,
                  gng_ref, gnb_ref, ones_ref, lng_ref, lnb_ref, o_ref, *, alpha):
    x = x_ref[...]
    dm = x.shape[1]
    gates = jax.nn.sigmoid(_mm(x.astype(BF16), wgate_ref[...]))

    ys5 = ys5_ref[...].astype(BF16)
    y_s5 = _mm(ys5, glu1_ref[...]) * jax.nn.sigmoid(_mm(ys5, glu2_ref[...]))
    y_lru = _mm(ylru_ref[...].astype(BF16), lruo_ref[...])

    ones_bd = ones_ref[...]
    inv_n = 1.0 / HEAD_DIM
    y = yrw_ref[...]
    yc = y - _segsum(y, ones_bd) * inv_n
    var = _segsum(yc * yc, ones_bd) * inv_n
    y = yc * lax.rsqrt(var + RW_GN_EPS) * gng_ref[...] + gnb_ref[...]
    y = (y + bonus_ref[...]) * g_ref[...]
    y_rw = _mm(y.astype(BF16), rwo_ref[...])

    merged = (gates[:, 0:dm] * y_s5 + gates[:, dm:2 * dm] * y_lru + gates[:, 2 * dm:3 * dm] * y_rw)
    o_ref[...] = _layer_norm(alpha * x + _mm(merged.astype(BF16), mix_ref[...]),
                             lng_ref[...], lnb_ref[...])


def _merge(x, ys5, ylru, yrw, bonus, g, w_gate, glu1, glu2, lru_o, rw_o, mix, gn_g, gn_b, ones_bd,
           ln_g, ln_b, l, alpha, tm, bsz):
    seq = x.shape[0]
    dm = x.shape[1] // bsz
    wdt = ys5.shape[1] // bsz
    params = (w_gate, glu1, glu2, lru_o, rw_o, mix, gn_g, gn_b)
    return pl.pallas_call(
        functools.partial(_merge_kernel, alpha=alpha),
        grid=(seq // tm, bsz),
        in_specs=[_seq_spec(tm, dm)] + [_seq_spec(tm, wdt)] * 5 + [_layer_spec(p, l) for p in params]
                 + [_const_spec(ones_bd.shape), _layer_spec(ln_g, l), _layer_spec(ln_b, l)],
        out_specs=_seq_spec(tm, dm),
        out_shape=jax.ShapeDtypeStruct((seq, bsz * dm), F32),
        compiler_params=_cparams("parallel", "parallel"),
        name="merge_mix_ln",
    )(x, ys5, ylru, yrw, bonus, g, *params, ones_bd, ln_g, ln_b)


def _kv_kernel(mem_ref, w_ref, o_ref):
    o_ref[...] = _mm(mem_ref[...].astype(BF16), w_ref[...]).astype(BF16)


def _kv_proj(mem2, wkv, tn):
    rows, dm = mem2.shape
    depth, _, n = wkv.shape
    return pl.pallas_call(
        _kv_kernel,
        grid=(depth, n // tn),
        in_specs=[pl.BlockSpec((rows, dm), lambda l, j: (0, 0)),
                  pl.BlockSpec((None, dm, tn), lambda l, j: (l, 0, j))],
        out_specs=pl.BlockSpec((None, rows, tn), lambda l, j: (l, 0, j)),
        out_shape=jax.ShapeDtypeStruct((depth, rows, n), BF16),
        compiler_params=_cparams("parallel", "parallel"),
        name="xa_kv_proj",
    )(mem2, wkv)


def _xattn_kernel(x_ref, k_ref, v_ref, wq_ref, wo_ref, g_ref, b_ref, o_ref, *, alpha, heads):
    x = x_ref[...]
    dm = x.shape[1]
    hd = dm // heads
    q = _mm(x.astype(BF16), wq_ref[...]).astype(BF16)
    outs = []
    for h in range(heads):
        sl = slice(h * hd, (h + 1) * hd)
        s = lax.dot_general(q[:, sl], k_ref[:, sl], (((1,), (1,)), ((), ())),
                            preferred_element_type=F32) * (hd ** -0.5)
        e = jnp.exp(s - jnp.max(s, -1, keepdims=True))
        p = e / jnp.sum(e, -1, keepdims=True)
        outs.append(_mm(p.astype(BF16), v_ref[:, sl]))
    o = jnp.concatenate(outs, axis=1)
    o_ref[...] = _layer_norm(alpha * x + _mm(o.astype(BF16), wo_ref[...]), g_ref[...], b_ref[...])


def _xattn(x, kv, wq, wo, g, b, l, alpha, bsz, ts):
    s = x.shape[0]
    dm = x.shape[1] // bsz
    m = kv.shape[1] // bsz
    seq = pl.BlockSpec((ts, dm), lambda bi, i: (i, bi))
    return pl.pallas_call(
        functools.partial(_xattn_kernel, alpha=alpha, heads=XA_HEADS),
        grid=(bsz, s // ts),
        in_specs=[seq,
                  pl.BlockSpec((None, m, dm), lambda bi, i: (l, bi, 0)),
                  pl.BlockSpec((None, m, dm), lambda bi, i: (l, bi, 1)),
                  _layer_spec(wq, l), _layer_spec(wo, l), _layer_spec(g, l), _layer_spec(b, l)],
        out_specs=seq,
        out_shape=jax.ShapeDtypeStruct((s, bsz * dm), F32),
        compiler_params=_cparams("parallel", "parallel"),
        name="xattn_ln",
    )(x, kv, kv, wq, wo, g, b)


def _block_diag(w):
    depth, h, n, _ = w.shape
    eye = jnp.eye(h, dtype=w.dtype)
    return (w[:, :, :, None, :] * eye[None, :, None, :, None]).reshape(depth, h * n, h * n)


def _s5_discretise(lam_re, lam_im, log_dt, b_re, b_im, c_re, c_im):
    depth, g, p = lam_re.shape
    c = b_re.shape[-1]
    gb = LANES // c
    nq = g // gb
    dt = jnp.exp(log_dt)[..., None]
    mag = jnp.exp(lam_re * dt)
    a_re = mag * jnp.cos(lam_im * dt)
    a_im = mag * jnp.sin(lam_im * dt)
    den = lam_re * lam_re + lam_im * lam_im
    co_re = ((a_re - 1.0) * lam_re + a_im * lam_im) / den
    co_im = (a_im * lam_re - (a_re - 1.0) * lam_im) / den
    bb_re = co_re[..., None] * b_re - co_im[..., None] * b_im
    bb_im = co_re[..., None] * b_im + co_im[..., None] * b_re
    eye = jnp.eye(gb, dtype=F32)

    def b_layout(w):
        w = jnp.swapaxes(w.reshape(depth, nq, gb, p, c), 3, 4)
        return (w[:, :, :, :, None, :] * eye[None, None, :, None, :, None]).reshape(depth, nq, gb * c, gb * p)

    def c_layout(w):
        w = jnp.swapaxes(w.reshape(depth, nq, gb, c, p), 3, 4)
        return (w[:, :, :, :, None, :] * eye[None, None, :, None, :, None]).reshape(depth, nq, gb * p, gb * c)

    bq = jnp.concatenate([b_layout(bb_re), b_layout(bb_im)], axis=-1).astype(BF16)
    cq = jnp.concatenate([c_layout(c_re), c_layout(-c_im)], axis=-2).astype(BF16)
    rows8 = lambda a: jnp.broadcast_to(a.reshape(depth, 1, g * p), (depth, SUBLANES, g * p))
    return bq, cq, rows8(a_re), rows8(a_im)


def _tile(n, target):
    return min(n, target)


def kernel(x, mem, ffn1_wg, ffn1_wu, ffn1_wd, ln1_g, ln1_b, w_in, s5_lam_re, s5_lam_im, s5_log_dt, s5_b_re, s5_b_im, s5_c_re, s5_c_im, s5_d, s5_glu_w1, s5_glu_w2, lru_conv_w, lru_conv_b, lru_wa, lru_ba, lru_wx, lru_bx, lru_lambda, lru_w_out, rw_mu, rw_w0, rw_w_up, rw_a0, rw_a_up, rw_g_up, rw_k_k, rw_k_a, rw_r_k, rw_ln_g, rw_ln_b, rw_w_out, mix_w_out, ln2_g, ln2_b, xa_wq, xa_wkv, xa_wo, ln3_g, ln3_b, ffn2_wg, ffn2_wu, ffn2_wd, ln4_g, ln4_b):
    bsz, seq, dm = x.shape
    depth = w_in.shape[0]
    wdt = s5_d.shape[1]
    heads = wdt // HEAD_DIM
    alpha = (2 * depth) ** 0.25
    assert bsz == SUBLANES, "one time step must be one aligned 8-row group"
    t = bsz * seq

    bf = lambda w: w.astype(BF16)
    vec = lambda p: p.reshape(depth, 1, -1)

    w_in_b = bf(w_in)
    w_s5, w_lx, w_ly = w_in_b[:, :, 0:wdt], w_in_b[:, :, wdt:2 * wdt], w_in_b[:, :, 2 * wdt:3 * wdt]
    n_rw = rw_mu.shape[1]
    w_rw = w_in_b[:, :, 3 * wdt:3 * wdt + n_rw]
    w_gate = w_in_b[:, :, 3 * wdt + n_rw:]
    ffn1 = (bf(ffn1_wg), bf(ffn1_wu), bf(ffn1_wd), vec(ln1_g), vec(ln1_b))
    ffn2 = (bf(ffn2_wg), bf(ffn2_wu), bf(ffn2_wd), vec(ln4_g), vec(ln4_b))

    bq, cq, a_re, a_im = _s5_discretise(s5_lam_re, s5_lam_im, s5_log_dt, s5_b_re, s5_b_im, s5_c_re, s5_c_im)
    wa_bd, wx_bd = bf(_block_diag(lru_wa)), bf(_block_diag(lru_wx))

    kh_n = LANES // (bsz * heads)
    assert kh_n == 2 and HEAD_DIM % kh_n == 0, "lane layout b*16 + kh*8 + h needs batch * heads * 2 == 128"
    kp_n = HEAD_DIM // kh_n
    key_perm = np.array([h_ * HEAD_DIM + kh_ * kp_n + kp_
                         for kp_ in range(kp_n) for kh_ in range(kh_n) for h_ in range(heads)])
    val_perm = np.array([h_ * HEAD_DIM + v_ for v_ in range(HEAD_DIM) for h_ in range(heads)])
    val_rep_perm = np.array([h_ * HEAD_DIM + v_
                             for v_ in range(HEAD_DIM) for _ in range(kh_n) for h_ in range(heads)])

    def rw_layout(p):
        return jnp.concatenate([p[..., 0:wdt][..., key_perm], p[..., wdt:2 * wdt][..., key_perm],
                                p[..., 2 * wdt:3 * wdt][..., val_rep_perm], p[..., 2 * wdt:3 * wdt][..., val_perm],
                                p[..., 3 * wdt:]], axis=-1)

    w_rw_p = rw_layout(w_rw)
    mu_p = vec(rw_layout(rw_mu))
    zeros_dr = jnp.zeros((depth, RW_A_RANK, wdt), BF16)
    zeros_ar = jnp.zeros((depth, RW_DECAY_RANK, wdt), BF16)
    wup_pad = jnp.concatenate([bf(rw_w_up)[:, :, key_perm], zeros_dr], axis=1)
    aup_pad = jnp.concatenate([zeros_ar, bf(rw_a_up)[:, :, key_perm]], axis=1)
    gup_p = bf(rw_g_up)[:, :, val_perm]
    w0_p, a0_p = vec(rw_w0[:, key_perm]), vec(rw_a0[:, key_perm])
    kk_p, ka_p = vec(rw_k_k[:, key_perm]), vec(rw_k_a[:, key_perm])
    rk_p = vec(rw_r_k.reshape(depth, wdt)[:, key_perm])
    gn_g_p, gn_b_p = vec(rw_ln_g[:, val_perm]), vec(rw_ln_b[:, val_perm])
    rw_o = bf(rw_w_out)[:, val_perm, :]
    head_of = jnp.arange(wdt) % heads
    ones_h = (head_of[:, None] == head_of[None, :]).astype(BF16)
    lane = jnp.arange(LANES)
    grp = kh_n * heads
    sel_b = ((lane[None, :] < grp) & (lane[None, :] % heads == head_of[:, None])).astype(BF16)
    sel_k = ((lane[None, :] >= grp) & (lane[None, :] < 2 * grp)
             & (lane[None, :] % heads == head_of[:, None])).astype(BF16)

    kv = _kv_proj(mem.reshape(-1, dm), bf(xa_wkv), 512)
    wq_b, wo_b = bf(xa_wq), bf(xa_wo)
    glu1, glu2, lru_o, mix = bf(s5_glu_w1), bf(s5_glu_w2), bf(lru_w_out), bf(mix_w_out)

    tm = _tile(seq, 512)

    h = x
    for l in range(depth):
        h, hb = _ffn_ln(h, *ffn1, l, alpha, tm, bsz, batch_major_in=(l == 0), with_bf16_copy=True)

        h_rows = hb.reshape(t, dm)
        y_s5 = _s5_branch(h_rows, w_s5, bq, cq, a_re, a_im, vec(s5_d), l, _tile(t, 512))
        y_lru = _lru_branch(h_rows, w_lx, w_ly, lru_conv_w, vec(lru_conv_b), wa_bd, vec(lru_ba), wx_bd,
                            vec(lru_bx), vec(lru_lambda), l, _tile(t, 1024))
        k5, vs, g, bonus = _rw_prep(hb, w_rw_p, mu_p, w0_p, wup_pad, a0_p, aup_pad, gup_p, kk_p, ka_p, rk_p,
                                    ones_h, sel_b, sel_k, l, bsz, _tile(seq, 512))
        y_rw = _rw_scan(k5, vs, bsz, heads, _tile(seq, 64))

        h = _merge(h, y_s5.reshape(seq, bsz * wdt), y_lru.reshape(seq, bsz * wdt), y_rw, bonus, g,
                   w_gate, glu1, glu2, lru_o, rw_o, mix,
                   gn_g_p, gn_b_p, ones_h, vec(ln2_g), vec(ln2_b), l, alpha, tm, bsz)
        h = _xattn(h, kv, wq_b, wo_b, vec(ln3_g), vec(ln3_b), l, alpha, bsz, _tile(seq, 512))
        h = _ffn_ln(h, *ffn2, l, alpha, tm, bsz, batch_major_out=(l == depth - 1))
    return h
```

```python
import functools
import math

import jax
import jax.numpy as jnp
import numpy as np
from jax import lax
from jax.experimental import pallas as pl
from jax.experimental.pallas import tpu as pltpu

F32 = jnp.float32
BF16 = jnp.bfloat16

LN_EPS = 1e-5
RW_GN_EPS = 64e-5
LRU_C = 8.0
LRU_CONV = 4
S5_GROUP = 16
S5_STATE = 64
HEAD_DIM = 64
XA_HEADS = 4
RW_DECAY_RANK = 64
RW_A_RANK = 64
RW_GATE_RANK = 128

SUBLANES = 8
LANES = 128
VMEM_LIMIT_BYTES = 56 * 1024 * 1024
RELAYOUT_ROWS = 64


def _cparams(*sem):
    return pltpu.CompilerParams(dimension_semantics=sem, vmem_limit_bytes=VMEM_LIMIT_BYTES)


def _layer_spec(arr, l):
    tail = arr.shape[1:]
    nd = len(tail)
    return pl.BlockSpec((None,) + tail, lambda *_: (l,) + (0,) * nd, pipeline_mode=pl.Buffered(1))


def _layer_norm(y, g, b):
    mu = jnp.mean(y, -1, keepdims=True)
    yc = y - mu
    var = jnp.mean(yc * yc, -1, keepdims=True)
    return yc * lax.rsqrt(var + LN_EPS) * g + b


def _mm(a, b):
    return jnp.dot(a, b, preferred_element_type=F32)


def _head_sums(x, heads):
    blocks = x.shape[1] // LANES
    acc = x[:, 0:LANES]
    for j in range(1, blocks):
        acc = acc + x[:, j * LANES:(j + 1) * LANES]
    shift = heads
    while shift < LANES:
        acc = acc + pltpu.roll(acc, shift, 1)
        shift *= 2
    return jnp.concatenate([acc] * blocks, axis=1)


def _softplus(x):
    return jnp.maximum(x, 0.0) + jnp.log1p(jnp.exp(-jnp.abs(x)))


def _ffn_kernel(x_ref, wg_ref, wu_ref, wd_ref, g_ref, b_ref, o_ref, *maybe_ob_ref, alpha, f_chunk):
    x = x_ref[...]
    xb = x.astype(BF16)
    d_ff = wg_ref.shape[1]
    acc = jnp.zeros(x.shape, F32)
    for c0 in range(0, d_ff, f_chunk):
        hg = _mm(xb, wg_ref[:, c0:c0 + f_chunk])
        hu = _mm(xb, wu_ref[:, c0:c0 + f_chunk])
        h = hg * jax.nn.sigmoid(hg) * hu
        acc = acc + _mm(h.astype(BF16), wd_ref[c0:c0 + f_chunk, :])
    out = _layer_norm(alpha * x + 0.5 * acc, g_ref[...], b_ref[...])
    o_ref[...] = out
    for ob_ref in maybe_ob_ref:
        ob_ref[...] = out.astype(ob_ref.dtype)


def _seq_spec(tm, n):
    return pl.BlockSpec((tm, n), lambda i, b: (i, b))


def _batch_major_spec(tm, n):
    return pl.BlockSpec((None, tm, n), lambda i, b: (b, i, 0))


def _ffn_ln(x, wg, wu, wd, g, b, l, alpha, tm, bsz, batch_major_in=False, batch_major_out=False,
            with_bf16_copy=False):
    d = wg.shape[1]
    seq = x.shape[1] if batch_major_in else x.shape[0]
    out_shape = (bsz, seq, d) if batch_major_out else (seq, bsz * d)
    out_spec = _batch_major_spec(tm, d) if batch_major_out else _seq_spec(tm, d)
    out_specs, out_shapes = out_spec, jax.ShapeDtypeStruct(out_shape, F32)
    if with_bf16_copy:
        out_specs = [out_spec, out_spec]
        out_shapes = [out_shapes, jax.ShapeDtypeStruct(out_shape, BF16)]
    return pl.pallas_call(
        functools.partial(_ffn_kernel, alpha=alpha, f_chunk=256),
        grid=(seq // tm, bsz),
        in_specs=[_batch_major_spec(tm, d) if batch_major_in else _seq_spec(tm, d),
                  _layer_spec(wg, l), _layer_spec(wu, l), _layer_spec(wd, l),
                  _layer_spec(g, l), _layer_spec(b, l)],
        out_specs=out_specs,
        out_shape=out_shapes,
        compiler_params=_cparams("parallel", "parallel"),
        name="ffn_ln",
    )(x, wg, wu, wd, g, b)


def _s5_kernel(x_ref, w_ref, bq_ref, cq_ref, are_ref, aim_ref, d_ref, o_ref,
               st_ref, carry_ref, u_ref):
    rows = x_ref.shape[0]
    nq = bq_ref.shape[0]
    half = bq_ref.shape[2] // 2
    cw = bq_ref.shape[1]

    @pl.when(pl.program_id(0) == 0)
    def _():
        carry_ref[...] = jnp.zeros_like(carry_ref)

    u = _mm(x_ref[...].astype(BF16), w_ref[...])
    u_ref[...] = u
    ub = u.astype(BF16)
    for q in range(nq):
        st_ref[:, 2 * half * q:2 * half * (q + 1)] = _mm(ub[:, cw * q:cw * (q + 1)], bq_ref[q])

    def step(t, s):
        r0 = pl.multiple_of(t * SUBLANES, SUBLANES)
        parts = []
        for q in range(nq):
            re = s[:, 2 * half * q:2 * half * q + half]
            im = s[:, 2 * half * q + half:2 * half * (q + 1)]
            ar = are_ref[:, half * q:half * (q + 1)]
            ai = aim_ref[:, half * q:half * (q + 1)]
            bre = st_ref[pl.ds(r0, SUBLANES), 2 * half * q:2 * half * q + half]
            bim = st_ref[pl.ds(r0, SUBLANES), 2 * half * q + half:2 * half * (q + 1)]
            parts.append(ar * re - ai * im + bre)
            parts.append(ar * im + ai * re + bim)
        s_new = jnp.concatenate(parts, axis=1)
        st_ref[pl.ds(r0, SUBLANES), :] = s_new
        return s_new

    carry_ref[...] = lax.fori_loop(0, rows // SUBLANES, step, carry_ref[...])

    for q in range(nq):
        y = _mm(st_ref[:, 2 * half * q:2 * half * (q + 1)].astype(BF16), cq_ref[q])
        y = y + d_ref[:, cw * q:cw * (q + 1)] * u_ref[:, cw * q:cw * (q + 1)]
        o_ref[:, cw * q:cw * (q + 1)] = jax.nn.gelu(y).astype(o_ref.dtype)


def _s5_branch(x, w_s5, bq, cq, a_re, a_im, d, l, rows):
    t, dm = x.shape
    wdt = w_s5.shape[2]
    n_state = a_re.shape[2]
    return pl.pallas_call(
        _s5_kernel,
        grid=(t // rows,),
        in_specs=[pl.BlockSpec((rows, dm), lambda i: (i, 0)),
                  _layer_spec(w_s5, l), _layer_spec(bq, l), _layer_spec(cq, l),
                  _layer_spec(a_re, l), _layer_spec(a_im, l), _layer_spec(d, l)],
        out_specs=pl.BlockSpec((rows, wdt), lambda i: (i, 0)),
        out_shape=jax.ShapeDtypeStruct((t, wdt), BF16),
        scratch_shapes=[pltpu.VMEM((rows, 2 * n_state), F32),
                        pltpu.VMEM((SUBLANES, 2 * n_state), F32),
                        pltpu.VMEM((rows, wdt), F32)],
        compiler_params=_cparams("arbitrary"),
        name="s5_branch",
    )(x, w_s5, bq, cq, a_re, a_im, d)


def _lru_kernel(x_ref, wlx_ref, wly_ref, cw_ref, cb_ref, wa_ref, ba_ref, wx_ref, bx_ref,
                lam_ref, o_ref, hist_ref, h_ref, a_ref, b_ref):
    rows = x_ref.shape[0]
    hist_rows = hist_ref.shape[0]

    @pl.when(pl.program_id(0) == 0)
    def _():
        hist_ref[...] = jnp.zeros_like(hist_ref)
        h_ref[...] = jnp.zeros_like(h_ref)

    xb = x_ref[...].astype(BF16)
    zx = _mm(xb, wlx_ref[...])
    zy = _mm(xb, wly_ref[...])

    ext = jnp.concatenate([hist_ref[...], zx], axis=0)
    hist_ref[...] = zx[rows - hist_rows:, :]
    xc = cb_ref[...] + cw_ref[LRU_CONV - 1:LRU_CONV, :] * zx
    for j in range(1, LRU_CONV):
        off = hist_rows - SUBLANES * j
        xc = xc + cw_ref[LRU_CONV - 1 - j:LRU_CONV - j, :] * ext[off:off + rows, :]

    xcb = xc.astype(BF16)
    gate_r = jax.nn.sigmoid(_mm(xcb, wa_ref[...]) + ba_ref[...])
    gate_i = jax.nn.sigmoid(_mm(xcb, wx_ref[...]) + bx_ref[...])
    log_a = -LRU_C * gate_r * _softplus(-lam_ref[...])
    a = jnp.exp(log_a)
    a_ref[...] = a
    b_ref[...] = jnp.sqrt(-jnp.tanh(log_a) * (1.0 + a * a)) * gate_i * xc

    def step(t, h):
        r0 = pl.multiple_of(t * SUBLANES, SUBLANES)
        h = a_ref[pl.ds(r0, SUBLANES), :] * h + b_ref[pl.ds(r0, SUBLANES), :]
        b_ref[pl.ds(r0, SUBLANES), :] = h
        return h

    h_ref[...] = lax.fori_loop(0, rows // SUBLANES, step, h_ref[...], unroll=8)
    o_ref[...] = (b_ref[...] * jax.nn.gelu(zy)).astype(o_ref.dtype)


def _lru_branch(x, w_lx, w_ly, conv_w, conv_b, wa_bd, ba, wx_bd, bx, lam, l, rows):
    t, dm = x.shape
    wdt = w_lx.shape[2]
    params = (w_lx, w_ly, conv_w, conv_b, wa_bd, ba, wx_bd, bx, lam)
    return pl.pallas_call(
        _lru_kernel,
        grid=(t // rows,),
        in_specs=[pl.BlockSpec((rows, dm), lambda i: (i, 0))] + [_layer_spec(p, l) for p in params],
        out_specs=pl.BlockSpec((rows, wdt), lambda i: (i, 0)),
        out_shape=jax.ShapeDtypeStruct((t, wdt), BF16),
        scratch_shapes=[pltpu.VMEM(((LRU_CONV - 1) * SUBLANES, wdt), F32),
                        pltpu.VMEM((SUBLANES, wdt), F32),
                        pltpu.VMEM((rows, wdt), F32),
                        pltpu.VMEM((rows, wdt), F32)],
        compiler_params=_cparams("arbitrary"),
        name="lru_branch",
    )(x, *params)


def _swap_tile_and_lane_group(tiles, grp):
    n = len(tiles)
    lane_g = lax.broadcasted_iota(jnp.int32, tiles[0].shape, 1) // grp
    k = 1
    while k < n:
        high = (lane_g & k) != 0
        nxt = list(tiles)
        for p in range(n):
            if p & k == 0:
                lo_t, hi_t = tiles[p], tiles[p + k]
                nxt[p] = jnp.where(high, pltpu.roll(hi_t, k * grp, 1), lo_t)
                nxt[p + k] = jnp.where(high, hi_t, pltpu.roll(lo_t, LANES - k * grp, 1))
        tiles = nxt
        k *= 2
    return tiles


def _rw_prep_kernel(x_ref, w_ref, mu_ref, w0_ref, wup_ref, a0_ref, aup_ref, gup_ref,
                    kk_ref, ka_ref, rk_ref,
                    k5_ref, vs_ref, g_ref, bonus_ref, prev_ref):
    tt = x_ref.shape[0]
    cols = w_ref.shape[1]
    wdt = w0_ref.shape[1]
    lora0 = 5 * wdt
    lora1 = lora0 + wup_ref.shape[0]
    b = pl.program_id(1)

    @pl.when(pl.program_id(0) == 0)
    def _():
        prev_ref[b] = jnp.zeros(prev_ref.shape[1:], F32)

    z = _mm(x_ref[...].astype(BF16), w_ref[...])
    first_row = lax.broadcasted_iota(jnp.int32, (tt, cols), 0) == 0
    shifted = jnp.where(first_row, prev_ref[b, 0:1, :], pltpu.roll(z, 1, 0))
    prev_ref[b, 0:1, :] = z[tt - 1:tt, :]
    z = z + mu_ref[...] * (shifted - z)

    r = z[:, 0:wdt]
    k = z[:, wdt:2 * wdt]
    v_rep = z[:, 2 * wdt:4 * wdt]
    v = z[:, 4 * wdt:5 * wdt]
    lora = z[:, lora0:lora1]
    gd = z[:, lora1:]

    w_log = -_softplus(-(w0_ref[...] + _mm(jnp.tanh(lora).astype(BF16), wup_ref[...]))) - 0.5
    decay = jnp.exp(-jnp.exp(w_log))
    a = jax.nn.sigmoid(a0_ref[...] + _mm(lora.astype(BF16), aup_ref[...]))
    g_ref[...] = _mm(jax.nn.sigmoid(gd).astype(BF16), gup_ref[...])

    heads = wdt // HEAD_DIM
    kk = k * kk_ref[...]
    kk = kk * lax.rsqrt(_head_sums(kk * kk, heads) + 1e-12)
    k = k * (1.0 + (a - 1.0) * ka_ref[...])
    kka = kk * a

    k5_ref[:, 0:wdt] = -kk
    k5_ref[:, wdt:2 * wdt] = decay
    k5_ref[:, 2 * wdt:3 * wdt] = kka
    k5_ref[:, 3 * wdt:4 * wdt] = k
    k5_ref[:, 4 * wdt:5 * wdt] = decay * r
    vs_ref[:, 0:2 * wdt] = v_rep
    first_group = lax.broadcasted_iota(jnp.int32, (tt, LANES), 1) < LANES // prev_ref.shape[0]
    vs_ref[:, 2 * wdt:2 * wdt + LANES] = jnp.where(first_group, _head_sums(kka * r, heads)[:, 0:LANES],
                                                   _head_sums(k * r, heads)[:, 0:LANES])
    bonus_ref[...] = _head_sums(r * k * rk_ref[...], heads) * v


def _rw_prep(x, w_rw, mu, w0, wup_pad, a0, aup_pad, gup, k_k, k_a, r_k, l, bsz, tt):
    seq = x.shape[0]
    dm = x.shape[1] // bsz
    wdt = w0.shape[2]
    cols = w_rw.shape[2]
    params = (w_rw, mu, w0, wup_pad, a0, aup_pad, gup, k_k, k_a, r_k)
    per_batch = lambda n: _seq_spec(tt, n)
    widths = (5 * wdt, 2 * wdt + LANES, wdt, wdt)
    return pl.pallas_call(
        _rw_prep_kernel,
        grid=(seq // tt, bsz),
        in_specs=[per_batch(dm)] + [_layer_spec(p, l) for p in params],
        out_specs=[per_batch(n) for n in widths],
        out_shape=[jax.ShapeDtypeStruct((seq, bsz * n), F32) for n in widths],
        scratch_shapes=[pltpu.VMEM((bsz, SUBLANES, cols), F32)],
        compiler_params=_cparams("arbitrary", "arbitrary"),
        name="rw_prep",
    )(x, *params)


def _rw_scan_kernel(k5_ref, vs_ref, y_ref, st_ref, k5p_ref, v2x_ref, ypl_ref, *, bsz, heads):
    tt = k5_ref.shape[0]
    kp_n = st_ref.shape[0]
    hd = st_ref.shape[1]
    wdt = hd * heads
    grp = LANES // bsz
    k5w = k5_ref.shape[1] // bsz
    vsw = vs_ref.shape[1] // bsz
    vt = hd // SUBLANES
    @pl.when(pl.program_id(0) == 0)
    def _():
        st_ref[...] = jnp.zeros_like(st_ref)

    rc = min(tt, RELAYOUT_ROWS)

    def batch_tiles(src_ref, width, blk, r0):
        return [src_ref[r0:r0 + rc, pl.ds(pl.multiple_of(b * width + blk * LANES, LANES), LANES)]
                for b in range(bsz)]

    def k5_block(blk, carry):
        for r0 in range(0, tt, rc):
            for i, tile in enumerate(_swap_tile_and_lane_group(batch_tiles(k5_ref, k5w, blk, r0), grp)):
                k5p_ref[blk * bsz + i, r0:r0 + rc, :] = tile
        return carry

    def vs_block(blk, carry):
        for r0 in range(0, tt, rc):
            for i, tile in enumerate(_swap_tile_and_lane_group(batch_tiles(vs_ref, vsw, blk, r0), grp)):
                v2x_ref[blk, r0:r0 + rc, i, :] = tile
        return carry

    lax.fori_loop(0, k5w // LANES, k5_block, 0, unroll=4)
    lax.fori_loop(0, vsw // LANES, vs_block, 0, unroll=3)

    low_half = (lax.broadcasted_iota(jnp.int32, (hd, LANES), 1) // heads) % 2 == 0

    def fold(p):
        return p + jnp.where(low_half, pltpu.roll(p, LANES - heads, 1), pltpu.roll(p, heads, 1))

    def row(j, kp, t):
        return k5p_ref[j * kp_n + kp, pl.ds(t, 1), :]

    def step(t, carry):
        vv = jnp.concatenate([v2x_ref[m, t] for m in range(vt)], axis=0)
        b_r = v2x_ref[vt, t, 0:1, :]
        k_r = v2x_ref[vt, t, 1:2, :]
        sa = jnp.zeros((hd, LANES), F32)
        yp = jnp.zeros((hd, LANES), F32)
        for kp in range(kp_n):
            s_k = st_ref[kp]
            sa = sa + s_k * row(0, kp, t)
            yp = yp + s_k * row(4, kp, t)
        sa = fold(sa)
        yp = fold(yp)
        for kp in range(kp_n):
            st_ref[kp] = st_ref[kp] * row(1, kp, t) + sa * row(2, kp, t) + vv * row(3, kp, t)
        y = yp + sa * b_r + vv * k_r
        for v in range(hd):
            ypl_ref[v, pl.ds(t, 1), :] = y[v:v + 1, :]
        return carry

    lax.fori_loop(0, tt, step, 0)

    pair_lo = lax.broadcasted_iota(jnp.int32, (rc, LANES), 1) % grp < heads

    def out_block(vb, carry):
        for r0 in range(0, tt, rc):
            pairs = []
            for m in range(bsz):
                even = ypl_ref[vb * 2 * bsz + 2 * m, r0:r0 + rc, :]
                odd = ypl_ref[vb * 2 * bsz + 2 * m + 1, r0:r0 + rc, :]
                pairs.append(jnp.where(pair_lo, even, pltpu.roll(odd, heads, 1)))
            for b, tile in enumerate(_swap_tile_and_lane_group(pairs, grp)):
                y_ref[r0:r0 + rc, pl.ds(pl.multiple_of(b * wdt + vb * LANES, LANES), LANES)] = tile
        return carry

    lax.fori_loop(0, wdt // LANES, out_block, 0, unroll=2)


def _rw_scan(k5, vs, bsz, heads, tt):
    seq = k5.shape[0]
    hd = HEAD_DIM
    wdt = hd * heads
    n_k5 = k5.shape[1] // LANES
    n_vs = vs.shape[1] // bsz // LANES
    return pl.pallas_call(
        functools.partial(_rw_scan_kernel, bsz=bsz, heads=heads),
        grid=(seq // tt,),
        in_specs=[pl.BlockSpec((tt, k5.shape[1]), lambda i: (i, 0)),
                  pl.BlockSpec((tt, vs.shape[1]), lambda i: (i, 0))],
        out_specs=pl.BlockSpec((tt, bsz * wdt), lambda i: (i, 0)),
        out_shape=jax.ShapeDtypeStruct((seq, bsz * wdt), F32),
        scratch_shapes=[pltpu.VMEM((n_k5 // 5, hd, LANES), F32),
                        pltpu.VMEM((n_k5, tt, LANES), F32),
                        pltpu.VMEM((n_vs, tt, bsz, LANES), F32),
                        pltpu.VMEM((hd, tt, LANES), F32)],
        compiler_params=_cparams("arbitrary"),
        name="rw_scan",
    )(k5, vs)


def _merge_kernel(x_ref, ys5_ref, ylru_ref, yrw_ref, bonus_ref, g_ref,
                  wgate_ref, glu1_ref, glu2_ref, lruo_ref, rwo_ref, mix_ref,
                  gng_ref, gnb_ref, lng_ref, lnb_ref, o_ref, *, alpha):
    x = x_ref[...]
    dm = x.shape[1]
    gates = jax.nn.sigmoid(_mm(x.astype(BF16), wgate_ref[...]))

    ys5 = ys5_ref[...].astype(BF16)
    y_s5 = _mm(ys5, glu1_ref[...]) * jax.nn.sigmoid(_mm(ys5, glu2_ref[...]))
    y_lru = _mm(ylru_ref[...].astype(BF16), lruo_ref[...])

    inv_n = 1.0 / HEAD_DIM
    y = yrw_ref[...]
    heads = y.shape[1] // HEAD_DIM
    yc = y - _head_sums(y, heads) * inv_n
    var = _head_sums(yc * yc, heads) * inv_n
    y = yc * lax.rsqrt(var + RW_GN_EPS) * gng_ref[...] + gnb_ref[...]
    y = (y + bonus_ref[...]) * g_ref[...]
    y_rw = _mm(y.astype(BF16), rwo_ref[...])

    merged = (gates[:, 0:dm] * y_s5 + gates[:, dm:2 * dm] * y_lru + gates[:, 2 * dm:3 * dm] * y_rw)
    o_ref[...] = _layer_norm(alpha * x + _mm(merged.astype(BF16), mix_ref[...]),
                             lng_ref[...], lnb_ref[...])


def _merge(x, ys5, ylru, yrw, bonus, g, w_gate, glu1, glu2, lru_o, rw_o, mix, gn_g, gn_b,
           ln_g, ln_b, l, alpha, tm, bsz):
    seq = x.shape[0]
    dm = x.shape[1] // bsz
    wdt = ys5.shape[1] // bsz
    params = (w_gate, glu1, glu2, lru_o, rw_o, mix, gn_g, gn_b)
    return pl.pallas_call(
        functools.partial(_merge_kernel, alpha=alpha),
        grid=(seq // tm, bsz),
        in_specs=[_seq_spec(tm, dm)] + [_seq_spec(tm, wdt)] * 5 + [_layer_spec(p, l) for p in params]
                 + [_layer_spec(ln_g, l), _layer_spec(ln_b, l)],
        out_specs=_seq_spec(tm, dm),
        out_shape=jax.ShapeDtypeStruct((seq, bsz * dm), F32),
        compiler_params=_cparams("parallel", "parallel"),
        name="merge_mix_ln",
    )(x, ys5, ylru, yrw, bonus, g, *params, ln_g, ln_b)


def _kv_kernel(mem_ref, w_ref, o_ref):
    o_ref[...] = _mm(mem_ref[...].astype(BF16), w_ref[...]).astype(BF16)


def _kv_proj(mem2, wkv, tn):
    rows, dm = mem2.shape
    depth, _, n = wkv.shape
    return pl.pallas_call(
        _kv_kernel,
        grid=(depth, n // tn),
        in_specs=[pl.BlockSpec((rows, dm), lambda l, j: (0, 0)),
                  pl.BlockSpec((None, dm, tn), lambda l, j: (l, 0, j))],
        out_specs=pl.BlockSpec((None, rows, tn), lambda l, j: (l, 0, j)),
        out_shape=jax.ShapeDtypeStruct((depth, rows, n), BF16),
        compiler_params=_cparams("parallel", "parallel"),
        name="xa_kv_proj",
    )(mem2, wkv)


def _xattn_kernel(x_ref, k_ref, v_ref, wq_ref, wo_ref, g_ref, b_ref, o_ref, *, alpha, heads):
    x = x_ref[...]
    dm = x.shape[1]
    hd = dm // heads
    q = _mm(x.astype(BF16), wq_ref[...]).astype(BF16)
    outs = []
    for h in range(heads):
        sl = slice(h * hd, (h + 1) * hd)
        s = lax.dot_general(q[:, sl], k_ref[:, sl], (((1,), (1,)), ((), ())),
                            preferred_element_type=F32) * (hd ** -0.5)
        e = jnp.exp(s - jnp.max(s, -1, keepdims=True))
        p = e / jnp.sum(e, -1, keepdims=True)
        outs.append(_mm(p.astype(BF16), v_ref[:, sl]))
    o = jnp.concatenate(outs, axis=1)
    o_ref[...] = _layer_norm(alpha * x + _mm(o.astype(BF16), wo_ref[...]), g_ref[...], b_ref[...])


def _xattn(x, kv, wq, wo, g, b, l, alpha, bsz, ts):
    s = x.shape[0]
    dm = x.shape[1] // bsz
    m = kv.shape[1] // bsz
    seq = pl.BlockSpec((ts, dm), lambda bi, i: (i, bi))
    return pl.pallas_call(
        functools.partial(_xattn_kernel, alpha=alpha, heads=XA_HEADS),
        grid=(bsz, s // ts),
        in_specs=[seq,
                  pl.BlockSpec((None, m, dm), lambda bi, i: (l, bi, 0)),
                  pl.BlockSpec((None, m, dm), lambda bi, i: (l, bi, 1)),
                  _layer_spec(wq, l), _layer_spec(wo, l), _layer_spec(g, l), _layer_spec(b, l)],
        out_specs=seq,
        out_shape=jax.ShapeDtypeStruct((s, bsz * dm), F32),
        compiler_params=_cparams("parallel", "parallel"),
        name="xattn_ln",
    )(x, kv, kv, wq, wo, g, b)


def _block_diag(w):
    depth, h, n, _ = w.shape
    eye = jnp.eye(h, dtype=w.dtype)
    return (w[:, :, :, None, :] * eye[None, :, None, :, None]).reshape(depth, h * n, h * n)


def _s5_discretise(lam_re, lam_im, log_dt, b_re, b_im, c_re, c_im):
    depth, g, p = lam_re.shape
    c = b_re.shape[-1]
    gb = LANES // c
    nq = g // gb
    dt = jnp.exp(log_dt)[..., None]
    mag = jnp.exp(lam_re * dt)
    a_re = mag * jnp.cos(lam_im * dt)
    a_im = mag * jnp.sin(lam_im * dt)
    den = lam_re * lam_re + lam_im * lam_im
    co_re = ((a_re - 1.0) * lam_re + a_im * lam_im) / den
    co_im = (a_im * lam_re - (a_re - 1.0) * lam_im) / den
    bb_re = co_re[..., None] * b_re - co_im[..., None] * b_im
    bb_im = co_re[..., None] * b_im + co_im[..., None] * b_re
    eye = jnp.eye(gb, dtype=F32)

    def b_layout(w):
        w = jnp.swapaxes(w.reshape(depth, nq, gb, p, c), 3, 4)
        return (w[:, :, :, :, None, :] * eye[None, None, :, None, :, None]).reshape(depth, nq, gb * c, gb * p)

    def c_layout(w):
        w = jnp.swapaxes(w.reshape(depth, nq, gb, c, p), 3, 4)
        return (w[:, :, :, :, None, :] * eye[None, None, :, None, :, None]).reshape(depth, nq, gb * p, gb * c)

    bq = jnp.concatenate([b_layout(bb_re), b_layout(bb_im)], axis=-1).astype(BF16)
    cq = jnp.concatenate([c_layout(c_re), c_layout(-c_im)], axis=-2).astype(BF16)
    rows8 = lambda a: jnp.broadcast_to(a.reshape(depth, 1, g * p), (depth, SUBLANES, g * p))
    return bq, cq, rows8(a_re), rows8(a_im)


def _tile(n, target):
    return min(n, target)


def kernel(x, mem, ffn1_wg, ffn1_wu, ffn1_wd, ln1_g, ln1_b, w_in, s5_lam_re, s5_lam_im, s5_log_dt, s5_b_re, s5_b_im, s5_c_re, s5_c_im, s5_d, s5_glu_w1, s5_glu_w2, lru_conv_w, lru_conv_b, lru_wa, lru_ba, lru_wx, lru_bx, lru_lambda, lru_w_out, rw_mu, rw_w0, rw_w_up, rw_a0, rw_a_up, rw_g_up, rw_k_k, rw_k_a, rw_r_k, rw_ln_g, rw_ln_b, rw_w_out, mix_w_out, ln2_g, ln2_b, xa_wq, xa_wkv, xa_wo, ln3_g, ln3_b, ffn2_wg, ffn2_wu, ffn2_wd, ln4_g, ln4_b):
    bsz, seq, dm = x.shape
    depth = w_in.shape[0]
    wdt = s5_d.shape[1]
    heads = wdt // HEAD_DIM
    alpha = (2 * depth) ** 0.25
    assert bsz == SUBLANES, "one time step must be one aligned 8-row group"
    t = bsz * seq

    bf = lambda w: w.astype(BF16)
    vec = lambda p: p.reshape(depth, 1, -1)

    w_in_b = bf(w_in)
    w_s5, w_lx, w_ly = w_in_b[:, :, 0:wdt], w_in_b[:, :, wdt:2 * wdt], w_in_b[:, :, 2 * wdt:3 * wdt]
    n_rw = rw_mu.shape[1]
    w_rw = w_in_b[:, :, 3 * wdt:3 * wdt + n_rw]
    w_gate = w_in_b[:, :, 3 * wdt + n_rw:]
    ffn1 = (bf(ffn1_wg), bf(ffn1_wu), bf(ffn1_wd), vec(ln1_g), vec(ln1_b))
    ffn2 = (bf(ffn2_wg), bf(ffn2_wu), bf(ffn2_wd), vec(ln4_g), vec(ln4_b))

    bq, cq, a_re, a_im = _s5_discretise(s5_lam_re, s5_lam_im, s5_log_dt, s5_b_re, s5_b_im, s5_c_re, s5_c_im)
    wa_bd, wx_bd = bf(_block_diag(lru_wa)), bf(_block_diag(lru_wx))

    kh_n = LANES // (bsz * heads)
    assert kh_n == 2 and HEAD_DIM % kh_n == 0, "lane layout b*16 + kh*8 + h needs batch * heads * 2 == 128"
    kp_n = HEAD_DIM // kh_n
    key_perm = np.array([h_ * HEAD_DIM + kh_ * kp_n + kp_
                         for kp_ in range(kp_n) for kh_ in range(kh_n) for h_ in range(heads)])
    val_perm = np.array([h_ * HEAD_DIM + v_ for v_ in range(HEAD_DIM) for h_ in range(heads)])
    val_rep_perm = np.array([h_ * HEAD_DIM + v_
                             for v_ in range(HEAD_DIM) for _ in range(kh_n) for h_ in range(heads)])

    def rw_layout(p):
        return jnp.concatenate([p[..., 0:wdt][..., key_perm], p[..., wdt:2 * wdt][..., key_perm],
                                p[..., 2 * wdt:3 * wdt][..., val_rep_perm], p[..., 2 * wdt:3 * wdt][..., val_perm],
                                p[..., 3 * wdt:]], axis=-1)

    w_rw_p = rw_layout(w_rw)
    mu_p = vec(rw_layout(rw_mu))
    zeros_dr = jnp.zeros((depth, RW_A_RANK, wdt), BF16)
    zeros_ar = jnp.zeros((depth, RW_DECAY_RANK, wdt), BF16)
    wup_pad = jnp.concatenate([bf(rw_w_up)[:, :, key_perm], zeros_dr], axis=1)
    aup_pad = jnp.concatenate([zeros_ar, bf(rw_a_up)[:, :, key_perm]], axis=1)
    gup_p = bf(rw_g_up)[:, :, val_perm]
    w0_p, a0_p = vec(rw_w0[:, key_perm]), vec(rw_a0[:, key_perm])
    kk_p, ka_p = vec(rw_k_k[:, key_perm]), vec(rw_k_a[:, key_perm])
    rk_p = vec(rw_r_k.reshape(depth, wdt)[:, key_perm])
    gn_g_p, gn_b_p = vec(rw_ln_g[:, val_perm]), vec(rw_ln_b[:, val_perm])
    rw_o = bf(rw_w_out)[:, val_perm, :]

    kv = _kv_proj(mem.reshape(-1, dm), bf(xa_wkv), 512)
    wq_b, wo_b = bf(xa_wq), bf(xa_wo)
    glu1, glu2, lru_o, mix = bf(s5_glu_w1), bf(s5_glu_w2), bf(lru_w_out), bf(mix_w_out)

    tm = _tile(seq, 512)

    h = x
    for l in range(depth):
        h, hb = _ffn_ln(h, *ffn1, l, alpha, tm, bsz, batch_major_in=(l == 0), with_bf16_copy=True)

        h_rows = hb.reshape(t, dm)
        y_s5 = _s5_branch(h_rows, w_s5, bq, cq, a_re, a_im, vec(s5_d), l, _tile(t, 512))
        y_lru = _lru_branch(h_rows, w_lx, w_ly, lru_conv_w, vec(lru_conv_b), wa_bd, vec(lru_ba), wx_bd,
                            vec(lru_bx), vec(lru_lambda), l, _tile(t, 1024))
        k5, vs, g, bonus = _rw_prep(hb, w_rw_p, mu_p, w0_p, wup_pad, a0_p, aup_pad, gup_p, kk_p, ka_p, rk_p,
                                    l, bsz, _tile(seq, 512))
        y_rw = _rw_scan(k5, vs, bsz, heads, _tile(seq, 64))

        h = _merge(h, y_s5.reshape(seq, bsz * wdt), y_lru.reshape(seq, bsz * wdt), y_rw, bonus, g,
                   w_gate, glu1, glu2, lru_o, rw_o, mix,
                   gn_g_p, gn_b_p, vec(ln2_g), vec(ln2_b), l, alpha, tm, bsz)
        h = _xattn(h, kv, wq_b, wo_b, vec(ln3_g), vec(ln3_b), l, alpha, bsz, _tile(seq, 512))
        h = _ffn_ln(h, *ffn2, l, alpha, tm, bsz, batch_major_out=(l == depth - 1))
    return h
```

```python
import functools

import jax
import jax.numpy as jnp
import numpy as np
from jax import lax
from jax.experimental import pallas as pl
from jax.experimental.pallas import tpu as pltpu

F32 = jnp.float32
BF16 = jnp.bfloat16

LN_EPS = 1e-5
RW_GN_EPS = 64e-5
LRU_C = 8.0
LRU_CONV = 4
HEAD_DIM = 64
XA_HEADS = 4
RW_DECAY_RANK = 64
RW_A_RANK = 64

SUBLANES = 8
LANES = 128
VMEM_LIMIT_BYTES = 56 * 1024 * 1024
RELAYOUT_ROWS = 64


def _cparams(*sem):
    return pltpu.CompilerParams(dimension_semantics=sem, vmem_limit_bytes=VMEM_LIMIT_BYTES)


def _layer_spec(arr, l):
    tail = arr.shape[1:]
    nd = len(tail)
    return pl.BlockSpec((None,) + tail, lambda *_: (l,) + (0,) * nd, pipeline_mode=pl.Buffered(1))


def _layer_norm(y, g, b):
    mu = jnp.mean(y, -1, keepdims=True)
    yc = y - mu
    var = jnp.mean(yc * yc, -1, keepdims=True)
    return yc * lax.rsqrt(var + LN_EPS) * g + b


def _mm(a, b):
    return jnp.dot(a, b, preferred_element_type=F32)


def _head_sums(x, heads):
    blocks = x.shape[1] // LANES
    acc = x[:, 0:LANES]
    for j in range(1, blocks):
        acc = acc + x[:, j * LANES:(j + 1) * LANES]
    shift = heads
    while shift < LANES:
        acc = acc + pltpu.roll(acc, shift, 1)
        shift *= 2
    return jnp.concatenate([acc] * blocks, axis=1)


def _softplus(x):
    return jnp.maximum(x, 0.0) + jnp.log1p(jnp.exp(-jnp.abs(x)))


def _ffn_kernel(x_ref, wg_ref, wu_ref, wd_ref, g_ref, b_ref, o_ref, *maybe_ob_ref, alpha, f_chunk):
    x = x_ref[...]
    xb = x.astype(BF16)
    d_ff = wg_ref.shape[1]
    acc = jnp.zeros(x.shape, F32)
    for c0 in range(0, d_ff, f_chunk):
        hg = _mm(xb, wg_ref[:, c0:c0 + f_chunk])
        hu = _mm(xb, wu_ref[:, c0:c0 + f_chunk])
        h = hg * jax.nn.sigmoid(hg) * hu
        acc = acc + _mm(h.astype(BF16), wd_ref[c0:c0 + f_chunk, :])
    out = _layer_norm(alpha * x + 0.5 * acc, g_ref[...], b_ref[...])
    o_ref[...] = out
    for ob_ref in maybe_ob_ref:
        ob_ref[...] = out.astype(ob_ref.dtype)


def _seq_spec(tm, n):
    return pl.BlockSpec((tm, n), lambda i, b: (i, b))


def _batch_major_spec(tm, n):
    return pl.BlockSpec((None, tm, n), lambda i, b: (b, i, 0))


def _ffn_ln(x, wg, wu, wd, g, b, l, alpha, tm, bsz, batch_major_in=False, batch_major_out=False,
            with_bf16_copy=False):
    d = wg.shape[1]
    seq = x.shape[1] if batch_major_in else x.shape[0]
    out_shape = (bsz, seq, d) if batch_major_out else (seq, bsz * d)
    out_spec = _batch_major_spec(tm, d) if batch_major_out else _seq_spec(tm, d)
    out_specs, out_shapes = out_spec, jax.ShapeDtypeStruct(out_shape, F32)
    if with_bf16_copy:
        out_specs = [out_spec, out_spec]
        out_shapes = [out_shapes, jax.ShapeDtypeStruct(out_shape, BF16)]
    return pl.pallas_call(
        functools.partial(_ffn_kernel, alpha=alpha, f_chunk=256),
        grid=(seq // tm, bsz),
        in_specs=[_batch_major_spec(tm, d) if batch_major_in else _seq_spec(tm, d),
                  _layer_spec(wg, l), _layer_spec(wu, l), _layer_spec(wd, l),
                  _layer_spec(g, l), _layer_spec(b, l)],
        out_specs=out_specs,
        out_shape=out_shapes,
        compiler_params=_cparams("parallel", "parallel"),
        name="ffn_ln",
    )(x, wg, wu, wd, g, b)


def _s5_kernel(x_ref, w_ref, bq_ref, cq_ref, are_ref, aim_ref, d_ref, o_ref,
               st_ref, carry_ref, u_ref):
    rows = x_ref.shape[0]
    nq = bq_ref.shape[0]
    half = bq_ref.shape[2] // 2
    cw = bq_ref.shape[1]

    @pl.when(pl.program_id(0) == 0)
    def _():
        carry_ref[...] = jnp.zeros_like(carry_ref)

    u = _mm(x_ref[...].astype(BF16), w_ref[...])
    u_ref[...] = u
    ub = u.astype(BF16)
    for q in range(nq):
        st_ref[:, 2 * half * q:2 * half * (q + 1)] = _mm(ub[:, cw * q:cw * (q + 1)], bq_ref[q])

    def step(t, s):
        r0 = pl.multiple_of(t * SUBLANES, SUBLANES)
        parts = []
        for q in range(nq):
            re = s[:, 2 * half * q:2 * half * q + half]
            im = s[:, 2 * half * q + half:2 * half * (q + 1)]
            ar = are_ref[:, half * q:half * (q + 1)]
            ai = aim_ref[:, half * q:half * (q + 1)]
            bre = st_ref[pl.ds(r0, SUBLANES), 2 * half * q:2 * half * q + half]
            bim = st_ref[pl.ds(r0, SUBLANES), 2 * half * q + half:2 * half * (q + 1)]
            parts.append(ar * re - ai * im + bre)
            parts.append(ar * im + ai * re + bim)
        s_new = jnp.concatenate(parts, axis=1)
        st_ref[pl.ds(r0, SUBLANES), :] = s_new
        return s_new

    carry_ref[...] = lax.fori_loop(0, rows // SUBLANES, step, carry_ref[...])

    for q in range(nq):
        y = _mm(st_ref[:, 2 * half * q:2 * half * (q + 1)].astype(BF16), cq_ref[q])
        y = y + d_ref[:, cw * q:cw * (q + 1)] * u_ref[:, cw * q:cw * (q + 1)]
        o_ref[:, cw * q:cw * (q + 1)] = jax.nn.gelu(y).astype(o_ref.dtype)


def _s5_branch(x, w_s5, bq, cq, a_re, a_im, d, l, rows):
    t, dm = x.shape
    wdt = w_s5.shape[2]
    n_state = a_re.shape[2]
    return pl.pallas_call(
        _s5_kernel,
        grid=(t // rows,),
        in_specs=[pl.BlockSpec((rows, dm), lambda i: (i, 0)),
                  _layer_spec(w_s5, l), _layer_spec(bq, l), _layer_spec(cq, l),
                  _layer_spec(a_re, l), _layer_spec(a_im, l), _layer_spec(d, l)],
        out_specs=pl.BlockSpec((rows, wdt), lambda i: (i, 0)),
        out_shape=jax.ShapeDtypeStruct((t, wdt), BF16),
        scratch_shapes=[pltpu.VMEM((rows, 2 * n_state), F32),
                        pltpu.VMEM((SUBLANES, 2 * n_state), F32),
                        pltpu.VMEM((rows, wdt), F32)],
        compiler_params=_cparams("arbitrary"),
        name="s5_branch",
    )(x, w_s5, bq, cq, a_re, a_im, d)


def _lru_kernel(x_ref, wlx_ref, wly_ref, cw_ref, cb_ref, wa_ref, ba_ref, wx_ref, bx_ref,
                lam_ref, o_ref, hist_ref, h_ref, a_ref, b_ref):
    rows = x_ref.shape[0]
    hist_rows = hist_ref.shape[0]

    @pl.when(pl.program_id(0) == 0)
    def _():
        hist_ref[...] = jnp.zeros_like(hist_ref)
        h_ref[...] = jnp.zeros_like(h_ref)

    xb = x_ref[...].astype(BF16)
    zx = _mm(xb, wlx_ref[...])
    zy = _mm(xb, wly_ref[...])

    ext = jnp.concatenate([hist_ref[...], zx], axis=0)
    hist_ref[...] = zx[rows - hist_rows:, :]
    xc = cb_ref[...] + cw_ref[LRU_CONV - 1:LRU_CONV, :] * zx
    for j in range(1, LRU_CONV):
        off = hist_rows - SUBLANES * j
        xc = xc + cw_ref[LRU_CONV - 1 - j:LRU_CONV - j, :] * ext[off:off + rows, :]

    xcb = xc.astype(BF16)
    gate_r = jax.nn.sigmoid(_mm(xcb, wa_ref[...]) + ba_ref[...])
    gate_i = jax.nn.sigmoid(_mm(xcb, wx_ref[...]) + bx_ref[...])
    log_a = -LRU_C * gate_r * _softplus(-lam_ref[...])
    a = jnp.exp(log_a)
    a_ref[...] = a
    b_ref[...] = jnp.sqrt(-jnp.tanh(log_a) * (1.0 + a * a)) * gate_i * xc

    def step(t, h):
        r0 = pl.multiple_of(t * SUBLANES, SUBLANES)
        h = a_ref[pl.ds(r0, SUBLANES), :] * h + b_ref[pl.ds(r0, SUBLANES), :]
        b_ref[pl.ds(r0, SUBLANES), :] = h
        return h

    h_ref[...] = lax.fori_loop(0, rows // SUBLANES, step, h_ref[...], unroll=8)
    o_ref[...] = (b_ref[...] * jax.nn.gelu(zy)).astype(o_ref.dtype)


def _lru_branch(x, w_lx, w_ly, conv_w, conv_b, wa_bd, ba, wx_bd, bx, lam, l, rows):
    t, dm = x.shape
    wdt = w_lx.shape[2]
    params = (w_lx, w_ly, conv_w, conv_b, wa_bd, ba, wx_bd, bx, lam)
    return pl.pallas_call(
        _lru_kernel,
        grid=(t // rows,),
        in_specs=[pl.BlockSpec((rows, dm), lambda i: (i, 0))] + [_layer_spec(p, l) for p in params],
        out_specs=pl.BlockSpec((rows, wdt), lambda i: (i, 0)),
        out_shape=jax.ShapeDtypeStruct((t, wdt), BF16),
        scratch_shapes=[pltpu.VMEM(((LRU_CONV - 1) * SUBLANES, wdt), F32),
                        pltpu.VMEM((SUBLANES, wdt), F32),
                        pltpu.VMEM((rows, wdt), F32),
                        pltpu.VMEM((rows, wdt), F32)],
        compiler_params=_cparams("arbitrary"),
        name="lru_branch",
    )(x, *params)


def _swap_tile_and_lane_group(tiles, grp):
    n = len(tiles)
    lane_g = lax.broadcasted_iota(jnp.int32, tiles[0].shape, 1) // grp
    k = 1
    while k < n:
        high = (lane_g & k) != 0
        nxt = list(tiles)
        for p in range(n):
            if p & k == 0:
                lo_t, hi_t = tiles[p], tiles[p + k]
                nxt[p] = jnp.where(high, pltpu.roll(hi_t, k * grp, 1), lo_t)
                nxt[p + k] = jnp.where(high, hi_t, pltpu.roll(lo_t, LANES - k * grp, 1))
        tiles = nxt
        k *= 2
    return tiles


def _rw_prep_kernel(x_ref, w_ref, mu_ref, w0_ref, wup_ref, a0_ref, aup_ref, gup_ref,
                    kk_ref, ka_ref, rk_ref,
                    k5_ref, vs_ref, g_ref, bonus_ref, prev_ref):
    tt = x_ref.shape[0]
    cols = w_ref.shape[1]
    wdt = w0_ref.shape[1]
    lora0 = 5 * wdt
    lora1 = lora0 + wup_ref.shape[0]
    b = pl.program_id(1)

    @pl.when(pl.program_id(0) == 0)
    def _():
        prev_ref[b] = jnp.zeros(prev_ref.shape[1:], F32)

    z = _mm(x_ref[...].astype(BF16), w_ref[...])
    first_row = lax.broadcasted_iota(jnp.int32, (tt, cols), 0) == 0
    shifted = jnp.where(first_row, prev_ref[b, 0:1, :], pltpu.roll(z, 1, 0))
    prev_ref[b, 0:1, :] = z[tt - 1:tt, :]
    z = z + mu_ref[...] * (shifted - z)

    r = z[:, 0:wdt]
    k = z[:, wdt:2 * wdt]
    v_rep = z[:, 2 * wdt:4 * wdt]
    v = z[:, 4 * wdt:5 * wdt]
    lora = z[:, lora0:lora1]
    gd = z[:, lora1:]

    w_log = -_softplus(-(w0_ref[...] + _mm(jnp.tanh(lora).astype(BF16), wup_ref[...]))) - 0.5
    decay = jnp.exp(-jnp.exp(w_log))
    a = jax.nn.sigmoid(a0_ref[...] + _mm(lora.astype(BF16), aup_ref[...]))
    g_ref[...] = _mm(jax.nn.sigmoid(gd).astype(BF16), gup_ref[...])

    heads = wdt // HEAD_DIM
    kk = k * kk_ref[...]
    kk = kk * lax.rsqrt(_head_sums(kk * kk, heads) + 1e-12)
    k = k * (1.0 + (a - 1.0) * ka_ref[...])
    kka = kk * a

    k5_ref[:, 0:wdt] = -kk
    k5_ref[:, wdt:2 * wdt] = decay
    k5_ref[:, 2 * wdt:3 * wdt] = kka
    k5_ref[:, 3 * wdt:4 * wdt] = k
    k5_ref[:, 4 * wdt:5 * wdt] = decay * r
    vs_ref[:, 0:2 * wdt] = v_rep
    first_group = lax.broadcasted_iota(jnp.int32, (tt, LANES), 1) < LANES // prev_ref.shape[0]
    vs_ref[:, 2 * wdt:2 * wdt + LANES] = jnp.where(first_group, _head_sums(kka * r, heads)[:, 0:LANES],
                                                   _head_sums(k * r, heads)[:, 0:LANES])
    bonus_ref[...] = _head_sums(r * k * rk_ref[...], heads) * v


def _rw_prep(x, w_rw, mu, w0, wup_pad, a0, aup_pad, gup, k_k, k_a, r_k, l, bsz, tt):
    seq = x.shape[0]
    dm = x.shape[1] // bsz
    wdt = w0.shape[2]
    cols = w_rw.shape[2]
    params = (w_rw, mu, w0, wup_pad, a0, aup_pad, gup, k_k, k_a, r_k)
    per_batch = lambda n: _seq_spec(tt, n)
    widths = (5 * wdt, 2 * wdt + LANES, wdt, wdt)
    return pl.pallas_call(
        _rw_prep_kernel,
        grid=(seq // tt, bsz),
        in_specs=[per_batch(dm)] + [_layer_spec(p, l) for p in params],
        out_specs=[per_batch(n) for n in widths],
        out_shape=[jax.ShapeDtypeStruct((seq, bsz * n), F32) for n in widths],
        scratch_shapes=[pltpu.VMEM((bsz, SUBLANES, cols), F32)],
        compiler_params=_cparams("arbitrary", "arbitrary"),
        name="rw_prep",
    )(x, *params)


def _rw_scan_kernel(k5_ref, vs_ref, y_ref, st_ref, k5p_ref, v2x_ref, ypl_ref, *, bsz, heads):
    tt = k5_ref.shape[0]
    kp_n = st_ref.shape[0]
    hd = st_ref.shape[1]
    wdt = hd * heads
    grp = LANES // bsz
    k5w = k5_ref.shape[1] // bsz
    vsw = vs_ref.shape[1] // bsz
    vt = hd // SUBLANES
    @pl.when(pl.program_id(0) == 0)
    def _():
        st_ref[...] = jnp.zeros_like(st_ref)

    rc = min(tt, RELAYOUT_ROWS)

    def batch_tiles(src_ref, width, blk, r0):
        return [src_ref[r0:r0 + rc, pl.ds(pl.multiple_of(b * width + blk * LANES, LANES), LANES)]
                for b in range(bsz)]

    def k5_block(blk, carry):
        for r0 in range(0, tt, rc):
            for i, tile in enumerate(_swap_tile_and_lane_group(batch_tiles(k5_ref, k5w, blk, r0), grp)):
                k5p_ref[blk * bsz + i, r0:r0 + rc, :] = tile
        return carry

    def vs_block(blk, carry):
        for r0 in range(0, tt, rc):
            for i, tile in enumerate(_swap_tile_and_lane_group(batch_tiles(vs_ref, vsw, blk, r0), grp)):
                v2x_ref[blk, r0:r0 + rc, i, :] = tile
        return carry

    lax.fori_loop(0, k5w // LANES, k5_block, 0, unroll=4)
    lax.fori_loop(0, vsw // LANES, vs_block, 0, unroll=3)

    low_half = (lax.broadcasted_iota(jnp.int32, (hd, LANES), 1) // heads) % 2 == 0

    def fold(p):
        return p + jnp.where(low_half, pltpu.roll(p, LANES - heads, 1), pltpu.roll(p, heads, 1))

    def row(j, kp, t):
        return k5p_ref[j * kp_n + kp, pl.ds(t, 1), :]

    def step(t, carry):
        vv = jnp.concatenate([v2x_ref[m, t] for m in range(vt)], axis=0)
        b_r = v2x_ref[vt, t, 0:1, :]
        k_r = v2x_ref[vt, t, 1:2, :]
        sa = jnp.zeros((hd, LANES), F32)
        yp = jnp.zeros((hd, LANES), F32)
        for kp in range(kp_n):
            s_k = st_ref[kp]
            sa = sa + s_k * row(0, kp, t)
            yp = yp + s_k * row(4, kp, t)
        sa = fold(sa)
        yp = fold(yp)
        for kp in range(kp_n):
            st_ref[kp] = st_ref[kp] * row(1, kp, t) + sa * row(2, kp, t) + vv * row(3, kp, t)
        y = yp + sa * b_r + vv * k_r
        for v in range(hd):
            ypl_ref[v, pl.ds(t, 1), :] = y[v:v + 1, :]
        return carry

    lax.fori_loop(0, tt, step, 0)

    pair_lo = lax.broadcasted_iota(jnp.int32, (rc, LANES), 1) % grp < heads

    def out_block(vb, carry):
        for r0 in range(0, tt, rc):
            pairs = []
            for m in range(bsz):
                even = ypl_ref[vb * 2 * bsz + 2 * m, r0:r0 + rc, :]
                odd = ypl_ref[vb * 2 * bsz + 2 * m + 1, r0:r0 + rc, :]
                pairs.append(jnp.where(pair_lo, even, pltpu.roll(odd, heads, 1)))
            for b, tile in enumerate(_swap_tile_and_lane_group(pairs, grp)):
                y_ref[r0:r0 + rc, pl.ds(pl.multiple_of(b * wdt + vb * LANES, LANES), LANES)] = tile
        return carry

    lax.fori_loop(0, wdt // LANES, out_block, 0, unroll=2)


def _rw_scan(k5, vs, bsz, heads, tt):
    seq = k5.shape[0]
    hd = HEAD_DIM
    wdt = hd * heads
    n_k5 = k5.shape[1] // LANES
    n_vs = vs.shape[1] // bsz // LANES
    return pl.pallas_call(
        functools.partial(_rw_scan_kernel, bsz=bsz, heads=heads),
        grid=(seq // tt,),
        in_specs=[pl.BlockSpec((tt, k5.shape[1]), lambda i: (i, 0)),
                  pl.BlockSpec((tt, vs.shape[1]), lambda i: (i, 0))],
        out_specs=pl.BlockSpec((tt, bsz * wdt), lambda i: (i, 0)),
        out_shape=jax.ShapeDtypeStruct((seq, bsz * wdt), F32),
        scratch_shapes=[pltpu.VMEM((n_k5 // 5, hd, LANES), F32),
                        pltpu.VMEM((n_k5, tt, LANES), F32),
                        pltpu.VMEM((n_vs, tt, bsz, LANES), F32),
                        pltpu.VMEM((hd, tt, LANES), F32)],
        compiler_params=_cparams("arbitrary"),
        name="rw_scan",
    )(k5, vs)


def _xattn_rows(x, k_ref, v_ref, wq_ref, wo_ref, heads):
    dm = x.shape[1]
    hd = dm // heads
    q = _mm(x.astype(BF16), wq_ref[...]).astype(BF16)
    outs = []
    for h in range(heads):
        sl = slice(h * hd, (h + 1) * hd)
        s = lax.dot_general(q[:, sl], k_ref[:, sl], (((1,), (1,)), ((), ())),
                            preferred_element_type=F32) * (hd ** -0.5)
        e = jnp.exp(s - jnp.max(s, -1, keepdims=True))
        p = e / jnp.sum(e, -1, keepdims=True)
        outs.append(_mm(p.astype(BF16), v_ref[:, sl]))
    return _mm(jnp.concatenate(outs, axis=1).astype(BF16), wo_ref[...])


def _merge_xattn_kernel(x_ref, ys5_ref, ylru_ref, yrw_ref, bonus_ref, g_ref,
                        wgate_ref, glu1_ref, glu2_ref, lruo_ref, rwo_ref, mix_ref,
                        gng_ref, gnb_ref, ln2g_ref, ln2b_ref,
                        k_ref, v_ref, wq_ref, wo_ref, ln3g_ref, ln3b_ref, o_ref, *, alpha, xa_heads):
    x = x_ref[...]
    dm = x.shape[1]
    gates = jax.nn.sigmoid(_mm(x.astype(BF16), wgate_ref[...]))

    ys5 = ys5_ref[...].astype(BF16)
    y_s5 = _mm(ys5, glu1_ref[...]) * jax.nn.sigmoid(_mm(ys5, glu2_ref[...]))
    y_lru = _mm(ylru_ref[...].astype(BF16), lruo_ref[...])

    inv_n = 1.0 / HEAD_DIM
    y = yrw_ref[...]
    heads = y.shape[1] // HEAD_DIM
    yc = y - _head_sums(y, heads) * inv_n
    var = _head_sums(yc * yc, heads) * inv_n
    y = yc * lax.rsqrt(var + RW_GN_EPS) * gng_ref[...] + gnb_ref[...]
    y = (y + bonus_ref[...]) * g_ref[...]
    y_rw = _mm(y.astype(BF16), rwo_ref[...])

    merged = (gates[:, 0:dm] * y_s5 + gates[:, dm:2 * dm] * y_lru + gates[:, 2 * dm:3 * dm] * y_rw)
    h = _layer_norm(alpha * x + _mm(merged.astype(BF16), mix_ref[...]), ln2g_ref[...], ln2b_ref[...])
    o_ref[...] = _layer_norm(alpha * h + _xattn_rows(h, k_ref, v_ref, wq_ref, wo_ref, xa_heads),
                             ln3g_ref[...], ln3b_ref[...])


def _merge_xattn(x, ys5, ylru, yrw, bonus, g, w_gate, glu1, glu2, lru_o, rw_o, mix, gn_g, gn_b,
                 ln2_g, ln2_b, kv, wq, wo, ln3_g, ln3_b, l, alpha, tm, bsz):
    seq = x.shape[0]
    dm = x.shape[1] // bsz
    wdt = ys5.shape[1] // bsz
    m = kv.shape[1] // bsz
    rows = lambda n: pl.BlockSpec((tm, n), lambda b, i: (i, b))
    merge_params = (w_gate, glu1, glu2, lru_o, rw_o, mix, gn_g, gn_b, ln2_g, ln2_b)
    xa_params = (wq, wo, ln3_g, ln3_b)
    return pl.pallas_call(
        functools.partial(_merge_xattn_kernel, alpha=alpha, xa_heads=XA_HEADS),
        grid=(bsz, seq // tm),
        in_specs=[rows(dm)] + [rows(wdt)] * 5 + [_layer_spec(p, l) for p in merge_params]
                 + [pl.BlockSpec((None, m, dm), lambda b, i: (l, b, 0)),
                    pl.BlockSpec((None, m, dm), lambda b, i: (l, b, 1))]
                 + [_layer_spec(p, l) for p in xa_params],
        out_specs=rows(dm),
        out_shape=jax.ShapeDtypeStruct((seq, bsz * dm), F32),
        compiler_params=_cparams("parallel", "parallel"),
        name="merge_xattn_ln",
    )(x, ys5, ylru, yrw, bonus, g, *merge_params, kv, kv, *xa_params)


def _kv_kernel(mem_ref, w_ref, o_ref):
    o_ref[...] = _mm(mem_ref[...].astype(BF16), w_ref[...]).astype(BF16)


def _kv_proj(mem2, wkv, tn):
    rows, dm = mem2.shape
    depth, _, n = wkv.shape
    return pl.pallas_call(
        _kv_kernel,
        grid=(depth, n // tn),
        in_specs=[pl.BlockSpec((rows, dm), lambda l, j: (0, 0)),
                  pl.BlockSpec((None, dm, tn), lambda l, j: (l, 0, j))],
        out_specs=pl.BlockSpec((None, rows, tn), lambda l, j: (l, 0, j)),
        out_shape=jax.ShapeDtypeStruct((depth, rows, n), BF16),
        compiler_params=_cparams("parallel", "parallel"),
        name="xa_kv_proj",
    )(mem2, wkv)


def _block_diag(w):
    depth, h, n, _ = w.shape
    eye = jnp.eye(h, dtype=w.dtype)
    return (w[:, :, :, None, :] * eye[None, :, None, :, None]).reshape(depth, h * n, h * n)


def _s5_discretise(lam_re, lam_im, log_dt, b_re, b_im, c_re, c_im):
    depth, g, p = lam_re.shape
    c = b_re.shape[-1]
    gb = LANES // c
    nq = g // gb
    dt = jnp.exp(log_dt)[..., None]
    mag = jnp.exp(lam_re * dt)
    a_re = mag * jnp.cos(lam_im * dt)
    a_im = mag * jnp.sin(lam_im * dt)
    den = lam_re * lam_re + lam_im * lam_im
    co_re = ((a_re - 1.0) * lam_re + a_im * lam_im) / den
    co_im = (a_im * lam_re - (a_re - 1.0) * lam_im) / den
    bb_re = co_re[..., None] * b_re - co_im[..., None] * b_im
    bb_im = co_re[..., None] * b_im + co_im[..., None] * b_re
    eye = jnp.eye(gb, dtype=F32)

    def b_layout(w):
        w = jnp.swapaxes(w.reshape(depth, nq, gb, p, c), 3, 4)
        return (w[:, :, :, :, None, :] * eye[None, None, :, None, :, None]).reshape(depth, nq, gb * c, gb * p)

    def c_layout(w):
        w = jnp.swapaxes(w.reshape(depth, nq, gb, c, p), 3, 4)
        return (w[:, :, :, :, None, :] * eye[None, None, :, None, :, None]).reshape(depth, nq, gb * p, gb * c)

    bq = jnp.concatenate([b_layout(bb_re), b_layout(bb_im)], axis=-1).astype(BF16)
    cq = jnp.concatenate([c_layout(c_re), c_layout(-c_im)], axis=-2).astype(BF16)
    rows8 = lambda a: jnp.broadcast_to(a.reshape(depth, 1, g * p), (depth, SUBLANES, g * p))
    return bq, cq, rows8(a_re), rows8(a_im)


def _tile(n, target):
    return min(n, target)


def kernel(x, mem, ffn1_wg, ffn1_wu, ffn1_wd, ln1_g, ln1_b, w_in, s5_lam_re, s5_lam_im, s5_log_dt, s5_b_re, s5_b_im, s5_c_re, s5_c_im, s5_d, s5_glu_w1, s5_glu_w2, lru_conv_w, lru_conv_b, lru_wa, lru_ba, lru_wx, lru_bx, lru_lambda, lru_w_out, rw_mu, rw_w0, rw_w_up, rw_a0, rw_a_up, rw_g_up, rw_k_k, rw_k_a, rw_r_k, rw_ln_g, rw_ln_b, rw_w_out, mix_w_out, ln2_g, ln2_b, xa_wq, xa_wkv, xa_wo, ln3_g, ln3_b, ffn2_wg, ffn2_wu, ffn2_wd, ln4_g, ln4_b):
    bsz, seq, dm = x.shape
    depth = w_in.shape[0]
    wdt = s5_d.shape[1]
    heads = wdt // HEAD_DIM
    alpha = (2 * depth) ** 0.25
    assert bsz == SUBLANES, "one time step must be one aligned 8-row group"
    t = bsz * seq

    bf = lambda w: w.astype(BF16)
    vec = lambda p: p.reshape(depth, 1, -1)

    w_in_b = bf(w_in)
    w_s5, w_lx, w_ly = w_in_b[:, :, 0:wdt], w_in_b[:, :, wdt:2 * wdt], w_in_b[:, :, 2 * wdt:3 * wdt]
    n_rw = rw_mu.shape[1]
    w_rw = w_in_b[:, :, 3 * wdt:3 * wdt + n_rw]
    w_gate = w_in_b[:, :, 3 * wdt + n_rw:]
    ffn1 = (bf(ffn1_wg), bf(ffn1_wu), bf(ffn1_wd), vec(ln1_g), vec(ln1_b))
    ffn2 = (bf(ffn2_wg), bf(ffn2_wu), bf(ffn2_wd), vec(ln4_g), vec(ln4_b))

    bq, cq, a_re, a_im = _s5_discretise(s5_lam_re, s5_lam_im, s5_log_dt, s5_b_re, s5_b_im, s5_c_re, s5_c_im)
    wa_bd, wx_bd = bf(_block_diag(lru_wa)), bf(_block_diag(lru_wx))

    kh_n = LANES // (bsz * heads)
    assert kh_n == 2 and HEAD_DIM % kh_n == 0, "lane layout b*16 + kh*8 + h needs batch * heads * 2 == 128"
    kp_n = HEAD_DIM // kh_n
    key_perm = np.array([h_ * HEAD_DIM + kh_ * kp_n + kp_
                         for kp_ in range(kp_n) for kh_ in range(kh_n) for h_ in range(heads)])
    val_perm = np.array([h_ * HEAD_DIM + v_ for v_ in range(HEAD_DIM) for h_ in range(heads)])
    val_rep_perm = np.array([h_ * HEAD_DIM + v_
                             for v_ in range(HEAD_DIM) for _ in range(kh_n) for h_ in range(heads)])

    def rw_layout(p):
        return jnp.concatenate([p[..., 0:wdt][..., key_perm], p[..., wdt:2 * wdt][..., key_perm],
                                p[..., 2 * wdt:3 * wdt][..., val_rep_perm], p[..., 2 * wdt:3 * wdt][..., val_perm],
                                p[..., 3 * wdt:]], axis=-1)

    w_rw_p = rw_layout(w_rw)
    mu_p = vec(rw_layout(rw_mu))
    zeros_dr = jnp.zeros((depth, RW_A_RANK, wdt), BF16)
    zeros_ar = jnp.zeros((depth, RW_DECAY_RANK, wdt), BF16)
    wup_pad = jnp.concatenate([bf(rw_w_up)[:, :, key_perm], zeros_dr], axis=1)
    aup_pad = jnp.concatenate([zeros_ar, bf(rw_a_up)[:, :, key_perm]], axis=1)
    gup_p = bf(rw_g_up)[:, :, val_perm]
    w0_p, a0_p = vec(rw_w0[:, key_perm]), vec(rw_a0[:, key_perm])
    kk_p, ka_p = vec(rw_k_k[:, key_perm]), vec(rw_k_a[:, key_perm])
    rk_p = vec(rw_r_k.reshape(depth, wdt)[:, key_perm])
    gn_g_p, gn_b_p = vec(rw_ln_g[:, val_perm]), vec(rw_ln_b[:, val_perm])
    rw_o = bf(rw_w_out)[:, val_perm, :]

    kv = _kv_proj(mem.reshape(-1, dm), bf(xa_wkv), 512)
    wq_b, wo_b = bf(xa_wq), bf(xa_wo)
    glu1, glu2, lru_o, mix = bf(s5_glu_w1), bf(s5_glu_w2), bf(lru_w_out), bf(mix_w_out)

    tm = _tile(seq, 512)

    h = x
    for l in range(depth):
        h, hb = _ffn_ln(h, *ffn1, l, alpha, tm, bsz, batch_major_in=(l == 0), with_bf16_copy=True)

        h_rows = hb.reshape(t, dm)
        y_s5 = _s5_branch(h_rows, w_s5, bq, cq, a_re, a_im, vec(s5_d), l, _tile(t, 512))
        y_lru = _lru_branch(h_rows, w_lx, w_ly, lru_conv_w, vec(lru_conv_b), wa_bd, vec(lru_ba), wx_bd,
                            vec(lru_bx), vec(lru_lambda), l, _tile(t, 1024))
        k5, vs, g, bonus = _rw_prep(hb, w_rw_p, mu_p, w0_p, wup_pad, a0_p, aup_pad, gup_p, kk_p, ka_p, rk_p,
                                    l, bsz, _tile(seq, 512))
        y_rw = _rw_scan(k5, vs, bsz, heads, _tile(seq, 64))

        h = _merge_xattn(h, y_s5.reshape(seq, bsz * wdt), y_lru.reshape(seq, bsz * wdt), y_rw, bonus, g,
                         w_gate, glu1, glu2, lru_o, rw_o, mix, gn_g_p, gn_b_p, vec(ln2_g), vec(ln2_b),
                         kv, wq_b, wo_b, vec(ln3_g), vec(ln3_b), l, alpha, tm, bsz)
        h = _ffn_ln(h, *ffn2, l, alpha, tm, bsz, batch_major_out=(l == depth - 1))
    return h
```

```python
import functools

import jax
import jax.numpy as jnp
import numpy as np
from jax import lax
from jax.experimental import pallas as pl
from jax.experimental.pallas import tpu as pltpu

F32 = jnp.float32
BF16 = jnp.bfloat16

LN_EPS = 1e-5
RW_GN_EPS = 64e-5
LRU_C = 8.0
LRU_CONV = 4
HEAD_DIM = 64
XA_HEADS = 4
RW_DECAY_RANK = 64
RW_A_RANK = 64

SUBLANES = 8
LANES = 128
VMEM_LIMIT_BYTES = 56 * 1024 * 1024
RELAYOUT_ROWS = 64


def _cparams(*sem):
    return pltpu.CompilerParams(dimension_semantics=sem, vmem_limit_bytes=VMEM_LIMIT_BYTES)


def _layer_spec(arr, l):
    tail = arr.shape[1:]
    nd = len(tail)
    return pl.BlockSpec((None,) + tail, lambda *_: (l,) + (0,) * nd, pipeline_mode=pl.Buffered(1))


def _layer_norm(y, g, b):
    mu = jnp.mean(y, -1, keepdims=True)
    yc = y - mu
    var = jnp.mean(yc * yc, -1, keepdims=True)
    return yc * lax.rsqrt(var + LN_EPS) * g + b


def _mm(a, b):
    return jnp.dot(a, b, preferred_element_type=F32)


def _head_sums(x, heads):
    blocks = x.shape[1] // LANES
    acc = x[:, 0:LANES]
    for j in range(1, blocks):
        acc = acc + x[:, j * LANES:(j + 1) * LANES]
    shift = heads
    while shift < LANES:
        acc = acc + pltpu.roll(acc, shift, 1)
        shift *= 2
    return jnp.concatenate([acc] * blocks, axis=1)


def _softplus(x):
    return jnp.maximum(x, 0.0) + jnp.log1p(jnp.exp(-jnp.abs(x)))


def _ffn_kernel(x_ref, wg_ref, wu_ref, wd_ref, g_ref, b_ref, o_ref, *maybe_ob_ref, alpha, f_chunk):
    x = x_ref[...]
    xb = x.astype(BF16)
    d_ff = wg_ref.shape[1]
    acc = jnp.zeros(x.shape, F32)
    for c0 in range(0, d_ff, f_chunk):
        hg = _mm(xb, wg_ref[:, c0:c0 + f_chunk])
        hu = _mm(xb, wu_ref[:, c0:c0 + f_chunk])
        h = hg * jax.nn.sigmoid(hg) * hu
        acc = acc + _mm(h.astype(BF16), wd_ref[c0:c0 + f_chunk, :])
    out = _layer_norm(alpha * x + 0.5 * acc, g_ref[...], b_ref[...])
    o_ref[...] = out
    for ob_ref in maybe_ob_ref:
        ob_ref[...] = out.astype(ob_ref.dtype)


def _seq_spec(tm, n):
    return pl.BlockSpec((tm, n), lambda i, b: (i, b))


def _batch_major_spec(tm, n):
    return pl.BlockSpec((None, tm, n), lambda i, b: (b, i, 0))


def _ffn_ln(x, wg, wu, wd, g, b, l, alpha, tm, bsz, batch_major_in=False, batch_major_out=False,
            with_bf16_copy=False):
    d = wg.shape[1]
    seq = x.shape[1] if batch_major_in else x.shape[0]
    out_shape = (bsz, seq, d) if batch_major_out else (seq, bsz * d)
    out_spec = _batch_major_spec(tm, d) if batch_major_out else _seq_spec(tm, d)
    out_specs, out_shapes = out_spec, jax.ShapeDtypeStruct(out_shape, F32)
    if with_bf16_copy:
        out_specs = [out_spec, out_spec]
        out_shapes = [out_shapes, jax.ShapeDtypeStruct(out_shape, BF16)]
    return pl.pallas_call(
        functools.partial(_ffn_kernel, alpha=alpha, f_chunk=256),
        grid=(seq // tm, bsz),
        in_specs=[_batch_major_spec(tm, d) if batch_major_in else _seq_spec(tm, d),
                  _layer_spec(wg, l), _layer_spec(wu, l), _layer_spec(wd, l),
                  _layer_spec(g, l), _layer_spec(b, l)],
        out_specs=out_specs,
        out_shape=out_shapes,
        compiler_params=_cparams("parallel", "parallel"),
        name="ffn_ln",
    )(x, wg, wu, wd, g, b)


def _s5_kernel(x_ref, w_ref, bq_ref, cq_ref, are_ref, aim_ref, d_ref, o_ref,
               st_ref, carry_ref, u_ref):
    rows = x_ref.shape[0]
    nq = bq_ref.shape[0]
    half = bq_ref.shape[2] // 2
    cw = bq_ref.shape[1]

    @pl.when(pl.program_id(0) == 0)
    def _():
        carry_ref[...] = jnp.zeros_like(carry_ref)

    u = _mm(x_ref[...].astype(BF16), w_ref[...])
    u_ref[...] = u
    ub = u.astype(BF16)
    for q in range(nq):
        st_ref[:, 2 * half * q:2 * half * (q + 1)] = _mm(ub[:, cw * q:cw * (q + 1)], bq_ref[q])

    def step(t, s):
        r0 = pl.multiple_of(t * SUBLANES, SUBLANES)
        parts = []
        for q in range(nq):
            re = s[:, 2 * half * q:2 * half * q + half]
            im = s[:, 2 * half * q + half:2 * half * (q + 1)]
            ar = are_ref[:, half * q:half * (q + 1)]
            ai = aim_ref[:, half * q:half * (q + 1)]
            bre = st_ref[pl.ds(r0, SUBLANES), 2 * half * q:2 * half * q + half]
            bim = st_ref[pl.ds(r0, SUBLANES), 2 * half * q + half:2 * half * (q + 1)]
            parts.append(ar * re - ai * im + bre)
            parts.append(ar * im + ai * re + bim)
        s_new = jnp.concatenate(parts, axis=1)
        st_ref[pl.ds(r0, SUBLANES), :] = s_new
        return s_new

    carry_ref[...] = lax.fori_loop(0, rows // SUBLANES, step, carry_ref[...])

    for q in range(nq):
        y = _mm(st_ref[:, 2 * half * q:2 * half * (q + 1)].astype(BF16), cq_ref[q])
        y = y + d_ref[:, cw * q:cw * (q + 1)] * u_ref[:, cw * q:cw * (q + 1)]
        o_ref[:, cw * q:cw * (q + 1)] = jax.nn.gelu(y).astype(o_ref.dtype)


def _s5_branch(x, w_s5, bq, cq, a_re, a_im, d, l, rows):
    t, dm = x.shape
    wdt = w_s5.shape[2]
    n_state = a_re.shape[2]
    return pl.pallas_call(
        _s5_kernel,
        grid=(t // rows,),
        in_specs=[pl.BlockSpec((rows, dm), lambda i: (i, 0)),
                  _layer_spec(w_s5, l), _layer_spec(bq, l), _layer_spec(cq, l),
                  _layer_spec(a_re, l), _layer_spec(a_im, l), _layer_spec(d, l)],
        out_specs=pl.BlockSpec((rows, wdt), lambda i: (i, 0)),
        out_shape=jax.ShapeDtypeStruct((t, wdt), BF16),
        scratch_shapes=[pltpu.VMEM((rows, 2 * n_state), F32),
                        pltpu.VMEM((SUBLANES, 2 * n_state), F32),
                        pltpu.VMEM((rows, wdt), F32)],
        compiler_params=_cparams("arbitrary"),
        name="s5_branch",
    )(x, w_s5, bq, cq, a_re, a_im, d)


def _lru_kernel(x_ref, wlx_ref, wly_ref, cw_ref, cb_ref, wa_ref, ba_ref, wx_ref, bx_ref,
                lam_ref, o_ref, hist_ref, h_ref, a_ref, b_ref):
    rows = x_ref.shape[0]
    hist_rows = hist_ref.shape[0]

    @pl.when(pl.program_id(0) == 0)
    def _():
        hist_ref[...] = jnp.zeros_like(hist_ref)
        h_ref[...] = jnp.zeros_like(h_ref)

    xb = x_ref[...].astype(BF16)
    zx = _mm(xb, wlx_ref[...])
    zy = _mm(xb, wly_ref[...])

    ext = jnp.concatenate([hist_ref[...], zx], axis=0)
    hist_ref[...] = zx[rows - hist_rows:, :]
    xc = cb_ref[...] + cw_ref[LRU_CONV - 1:LRU_CONV, :] * zx
    for j in range(1, LRU_CONV):
        off = hist_rows - SUBLANES * j
        xc = xc + cw_ref[LRU_CONV - 1 - j:LRU_CONV - j, :] * ext[off:off + rows, :]

    xcb = xc.astype(BF16)
    gate_r = jax.nn.sigmoid(_mm(xcb, wa_ref[...]) + ba_ref[...])
    gate_i = jax.nn.sigmoid(_mm(xcb, wx_ref[...]) + bx_ref[...])
    log_a = -LRU_C * gate_r * _softplus(-lam_ref[...])
    a = jnp.exp(log_a)
    a_ref[...] = a
    b_ref[...] = jnp.sqrt(-jnp.tanh(log_a) * (1.0 + a * a)) * gate_i * xc

    def step(t, h):
        r0 = pl.multiple_of(t * SUBLANES, SUBLANES)
        h = a_ref[pl.ds(r0, SUBLANES), :] * h + b_ref[pl.ds(r0, SUBLANES), :]
        b_ref[pl.ds(r0, SUBLANES), :] = h
        return h

    h_ref[...] = lax.fori_loop(0, rows // SUBLANES, step, h_ref[...], unroll=8)
    o_ref[...] = (b_ref[...] * jax.nn.gelu(zy)).astype(o_ref.dtype)


def _lru_branch(x, w_lx, w_ly, conv_w, conv_b, wa_bd, ba, wx_bd, bx, lam, l, rows):
    t, dm = x.shape
    wdt = w_lx.shape[2]
    params = (w_lx, w_ly, conv_w, conv_b, wa_bd, ba, wx_bd, bx, lam)
    return pl.pallas_call(
        _lru_kernel,
        grid=(t // rows,),
        in_specs=[pl.BlockSpec((rows, dm), lambda i: (i, 0))] + [_layer_spec(p, l) for p in params],
        out_specs=pl.BlockSpec((rows, wdt), lambda i: (i, 0)),
        out_shape=jax.ShapeDtypeStruct((t, wdt), BF16),
        scratch_shapes=[pltpu.VMEM(((LRU_CONV - 1) * SUBLANES, wdt), F32),
                        pltpu.VMEM((SUBLANES, wdt), F32),
                        pltpu.VMEM((rows, wdt), F32),
                        pltpu.VMEM((rows, wdt), F32)],
        compiler_params=_cparams("arbitrary"),
        name="lru_branch",
    )(x, *params)


def _swap_tile_and_lane_group(tiles, grp):
    n = len(tiles)
    lane_g = lax.broadcasted_iota(jnp.int32, tiles[0].shape, 1) // grp
    k = 1
    while k < n:
        high = (lane_g & k) != 0
        nxt = list(tiles)
        for p in range(n):
            if p & k == 0:
                lo_t, hi_t = tiles[p], tiles[p + k]
                nxt[p] = jnp.where(high, pltpu.roll(hi_t, k * grp, 1), lo_t)
                nxt[p + k] = jnp.where(high, hi_t, pltpu.roll(lo_t, LANES - k * grp, 1))
        tiles = nxt
        k *= 2
    return tiles


def _rw_prep_kernel(x_ref, w_ref, mu_ref, w0_ref, wup_ref, a0_ref, aup_ref, gup_ref,
                    kk_ref, ka_ref, rk_ref,
                    k5_ref, vs_ref, g_ref, bonus_ref, prev_ref):
    tt = x_ref.shape[0]
    cols = w_ref.shape[1]
    wdt = w0_ref.shape[1]
    lora0 = 3 * wdt
    lora1 = lora0 + wup_ref.shape[0]
    b = pl.program_id(1)

    @pl.when(pl.program_id(0) == 0)
    def _():
        prev_ref[b] = jnp.zeros(prev_ref.shape[1:], F32)

    z = _mm(x_ref[...].astype(BF16), w_ref[...])
    first_row = lax.broadcasted_iota(jnp.int32, (tt, cols), 0) == 0
    shifted = jnp.where(first_row, prev_ref[b, 0:1, :], pltpu.roll(z, 1, 0))
    prev_ref[b, 0:1, :] = z[tt - 1:tt, :]
    z = z + mu_ref[...] * (shifted - z)

    r = z[:, 0:wdt]
    k = z[:, wdt:2 * wdt]
    v = z[:, 2 * wdt:3 * wdt]
    lora = z[:, lora0:lora1]
    gd = z[:, lora1:]

    w_log = -_softplus(-(w0_ref[...] + _mm(jnp.tanh(lora).astype(BF16), wup_ref[...]))) - 0.5
    decay = jnp.exp(-jnp.exp(w_log))
    a = jax.nn.sigmoid(a0_ref[...] + _mm(lora.astype(BF16), aup_ref[...]))
    g_ref[...] = _mm(jax.nn.sigmoid(gd).astype(BF16), gup_ref[...])

    heads = wdt // HEAD_DIM
    kk = k * kk_ref[...]
    kk = kk * lax.rsqrt(_head_sums(kk * kk, heads) + 1e-12)
    k = k * (1.0 + (a - 1.0) * ka_ref[...])
    kka = kk * a

    k5_ref[:, 0:wdt] = -kk
    k5_ref[:, wdt:2 * wdt] = decay
    k5_ref[:, 2 * wdt:3 * wdt] = kka
    k5_ref[:, 3 * wdt:4 * wdt] = k
    k5_ref[:, 4 * wdt:5 * wdt] = decay * r
    vs_ref[:, 0:wdt] = v
    first_group = lax.broadcasted_iota(jnp.int32, (tt, LANES), 1) < LANES // prev_ref.shape[0]
    vs_ref[:, wdt:wdt + LANES] = jnp.where(first_group, _head_sums(kka * r, heads)[:, 0:LANES],
                                         _head_sums(k * r, heads)[:, 0:LANES])
    bonus_ref[...] = _head_sums(r * k * rk_ref[...], heads) * v


def _rw_prep(x, w_rw, mu, w0, wup_pad, a0, aup_pad, gup, k_k, k_a, r_k, l, bsz, tt):
    seq = x.shape[0]
    dm = x.shape[1] // bsz
    wdt = w0.shape[2]
    cols = w_rw.shape[2]
    params = (w_rw, mu, w0, wup_pad, a0, aup_pad, gup, k_k, k_a, r_k)
    per_batch = lambda n: _seq_spec(tt, n)
    widths = (5 * wdt, wdt + LANES, wdt, wdt)
    return pl.pallas_call(
        _rw_prep_kernel,
        grid=(seq // tt, bsz),
        in_specs=[per_batch(dm)] + [_layer_spec(p, l) for p in params],
        out_specs=[per_batch(n) for n in widths],
        out_shape=[jax.ShapeDtypeStruct((seq, bsz * n), F32) for n in widths],
        scratch_shapes=[pltpu.VMEM((bsz, SUBLANES, cols), F32)],
        compiler_params=_cparams("arbitrary", "arbitrary"),
        name="rw_prep",
    )(x, *params)


def _rw_scan_kernel(k5_ref, vs_ref, y_ref, st_ref, k5p_ref, v2x_ref, ypl_ref, *, bsz, heads):
    tt = k5_ref.shape[0]
    kp_n = st_ref.shape[0]
    hd = st_ref.shape[1]
    wdt = hd * heads
    grp = LANES // bsz
    k5w = k5_ref.shape[1] // bsz
    vsw = vs_ref.shape[1] // bsz
    vt = hd // SUBLANES
    @pl.when(pl.program_id(0) == 0)
    def _():
        st_ref[...] = jnp.zeros_like(st_ref)

    rc = min(tt, RELAYOUT_ROWS)

    def batch_tiles(src_ref, width, blk, r0):
        return [src_ref[r0:r0 + rc, pl.ds(pl.multiple_of(b * width + blk * LANES, LANES), LANES)]
                for b in range(bsz)]

    def k5_block(blk, carry):
        for r0 in range(0, tt, rc):
            for i, tile in enumerate(_swap_tile_and_lane_group(batch_tiles(k5_ref, k5w, blk, r0), grp)):
                k5p_ref[blk * bsz + i, r0:r0 + rc, :] = tile
        return carry

    pair_lo = lax.broadcasted_iota(jnp.int32, (rc, LANES), 1) % grp < heads

    def v_block(vb, carry):
        for r0 in range(0, tt, rc):
            pairs = _swap_tile_and_lane_group(batch_tiles(vs_ref, vsw, vb, r0), grp)
            for m, pair in enumerate(pairs):
                rows = (jnp.where(pair_lo, pair, pltpu.roll(pair, heads, 1)),
                        jnp.where(pair_lo, pltpu.roll(pair, LANES - heads, 1), pair))
                for par, row_tile in enumerate(rows):
                    j = 2 * m + par
                    v2x_ref[vb * 2 + j // SUBLANES, r0:r0 + rc, j % SUBLANES, :] = row_tile
        return carry

    lax.fori_loop(0, k5w // LANES, k5_block, 0, unroll=4)
    lax.fori_loop(0, vt // 2, v_block, 0, unroll=2)
    for r0 in range(0, tt, rc):
        dots = _swap_tile_and_lane_group(batch_tiles(vs_ref, vsw, vt // 2, r0), grp)
        for i in range(2):
            v2x_ref[vt, r0:r0 + rc, i, :] = dots[i]

    low_half = (lax.broadcasted_iota(jnp.int32, (hd, LANES), 1) // heads) % 2 == 0

    def fold(p):
        return p + jnp.where(low_half, pltpu.roll(p, LANES - heads, 1), pltpu.roll(p, heads, 1))

    def row(j, kp, t):
        return k5p_ref[j * kp_n + kp, pl.ds(t, 1), :]

    def step(t, carry):
        vv = jnp.concatenate([v2x_ref[m, t] for m in range(vt)], axis=0)
        b_r = v2x_ref[vt, t, 0:1, :]
        k_r = v2x_ref[vt, t, 1:2, :]
        sa = jnp.zeros((hd, LANES), F32)
        yp = jnp.zeros((hd, LANES), F32)
        for kp in range(kp_n):
            s_k = st_ref[kp]
            sa = sa + s_k * row(0, kp, t)
            yp = yp + s_k * row(4, kp, t)
        sa = fold(sa)
        yp = fold(yp)
        for kp in range(kp_n):
            st_ref[kp] = st_ref[kp] * row(1, kp, t) + sa * row(2, kp, t) + vv * row(3, kp, t)
        y = yp + sa * b_r + vv * k_r
        for v in range(hd):
            ypl_ref[v, pl.ds(t, 1), :] = y[v:v + 1, :]
        return carry

    lax.fori_loop(0, tt, step, 0)


    def out_block(vb, carry):
        for r0 in range(0, tt, rc):
            pairs = []
            for m in range(bsz):
                even = ypl_ref[vb * 2 * bsz + 2 * m, r0:r0 + rc, :]
                odd = ypl_ref[vb * 2 * bsz + 2 * m + 1, r0:r0 + rc, :]
                pairs.append(jnp.where(pair_lo, even, pltpu.roll(odd, heads, 1)))
            for b, tile in enumerate(_swap_tile_and_lane_group(pairs, grp)):
                y_ref[r0:r0 + rc, pl.ds(pl.multiple_of(b * wdt + vb * LANES, LANES), LANES)] = tile
        return carry

    lax.fori_loop(0, wdt // LANES, out_block, 0, unroll=2)


def _rw_scan(k5, vs, bsz, heads, tt):
    seq = k5.shape[0]
    hd = HEAD_DIM
    wdt = hd * heads
    n_k5 = k5.shape[1] // LANES
    n_vs = hd // SUBLANES + 1
    return pl.pallas_call(
        functools.partial(_rw_scan_kernel, bsz=bsz, heads=heads),
        grid=(seq // tt,),
        in_specs=[pl.BlockSpec((tt, k5.shape[1]), lambda i: (i, 0)),
                  pl.BlockSpec((tt, vs.shape[1]), lambda i: (i, 0))],
        out_specs=pl.BlockSpec((tt, bsz * wdt), lambda i: (i, 0)),
        out_shape=jax.ShapeDtypeStruct((seq, bsz * wdt), F32),
        scratch_shapes=[pltpu.VMEM((n_k5 // 5, hd, LANES), F32),
                        pltpu.VMEM((n_k5, tt, LANES), F32),
                        pltpu.VMEM((n_vs, tt, bsz, LANES), F32),
                        pltpu.VMEM((hd, tt, LANES), F32)],
        compiler_params=_cparams("arbitrary"),
        name="rw_scan",
    )(k5, vs)


def _xattn_rows(x, k_ref, v_ref, wq_ref, wo_ref, heads):
    dm = x.shape[1]
    hd = dm // heads
    q = _mm(x.astype(BF16), wq_ref[...]).astype(BF16)
    outs = []
    for h in range(heads):
        sl = slice(h * hd, (h + 1) * hd)
        s = lax.dot_general(q[:, sl], k_ref[:, sl], (((1,), (1,)), ((), ())),
                            preferred_element_type=F32) * (hd ** -0.5)
        e = jnp.exp(s - jnp.max(s, -1, keepdims=True))
        p = e / jnp.sum(e, -1, keepdims=True)
        outs.append(_mm(p.astype(BF16), v_ref[:, sl]))
    return _mm(jnp.concatenate(outs, axis=1).astype(BF16), wo_ref[...])


def _merge_xattn_kernel(x_ref, ys5_ref, ylru_ref, yrw_ref, bonus_ref, g_ref,
                        wgate_ref, glu1_ref, glu2_ref, lruo_ref, rwo_ref, mix_ref,
                        gng_ref, gnb_ref, ln2g_ref, ln2b_ref,
                        k_ref, v_ref, wq_ref, wo_ref, ln3g_ref, ln3b_ref, o_ref, *, alpha, xa_heads):
    x = x_ref[...]
    dm = x.shape[1]
    gates = jax.nn.sigmoid(_mm(x.astype(BF16), wgate_ref[...]))

    ys5 = ys5_ref[...].astype(BF16)
    y_s5 = _mm(ys5, glu1_ref[...]) * jax.nn.sigmoid(_mm(ys5, glu2_ref[...]))
    y_lru = _mm(ylru_ref[...].astype(BF16), lruo_ref[...])

    inv_n = 1.0 / HEAD_DIM
    y = yrw_ref[...]
    heads = y.shape[1] // HEAD_DIM
    yc = y - _head_sums(y, heads) * inv_n
    var = _head_sums(yc * yc, heads) * inv_n
    y = yc * lax.rsqrt(var + RW_GN_EPS) * gng_ref[...] + gnb_ref[...]
    y = (y + bonus_ref[...]) * g_ref[...]
    y_rw = _mm(y.astype(BF16), rwo_ref[...])

    merged = (gates[:, 0:dm] * y_s5 + gates[:, dm:2 * dm] * y_lru + gates[:, 2 * dm:3 * dm] * y_rw)
    h = _layer_norm(alpha * x + _mm(merged.astype(BF16), mix_ref[...]), ln2g_ref[...], ln2b_ref[...])
    o_ref[...] = _layer_norm(alpha * h + _xattn_rows(h, k_ref, v_ref, wq_ref, wo_ref, xa_heads),
                             ln3g_ref[...], ln3b_ref[...])


def _merge_xattn(x, ys5, ylru, yrw, bonus, g, w_gate, glu1, glu2, lru_o, rw_o, mix, gn_g, gn_b,
                 ln2_g, ln2_b, kv, wq, wo, ln3_g, ln3_b, l, alpha, tm, bsz):
    seq = x.shape[0]
    dm = x.shape[1] // bsz
    wdt = ys5.shape[1] // bsz
    m = kv.shape[1] // bsz
    rows = lambda n: pl.BlockSpec((tm, n), lambda b, i: (i, b))
    merge_params = (w_gate, glu1, glu2, lru_o, rw_o, mix, gn_g, gn_b, ln2_g, ln2_b)
    xa_params = (wq, wo, ln3_g, ln3_b)
    return pl.pallas_call(
        functools.partial(_merge_xattn_kernel, alpha=alpha, xa_heads=XA_HEADS),
        grid=(bsz, seq // tm),
        in_specs=[rows(dm)] + [rows(wdt)] * 5 + [_layer_spec(p, l) for p in merge_params]
                 + [pl.BlockSpec((None, m, dm), lambda b, i: (l, b, 0)),
                    pl.BlockSpec((None, m, dm), lambda b, i: (l, b, 1))]
                 + [_layer_spec(p, l) for p in xa_params],
        out_specs=rows(dm),
        out_shape=jax.ShapeDtypeStruct((seq, bsz * dm), F32),
        compiler_params=_cparams("parallel", "parallel"),
        name="merge_xattn_ln",
    )(x, ys5, ylru, yrw, bonus, g, *merge_params, kv, kv, *xa_params)


def _kv_kernel(mem_ref, w_ref, o_ref):
    o_ref[...] = _mm(mem_ref[...].astype(BF16), w_ref[...]).astype(BF16)


def _kv_proj(mem2, wkv, tn):
    rows, dm = mem2.shape
    depth, _, n = wkv.shape
    return pl.pallas_call(
        _kv_kernel,
        grid=(depth, n // tn),
        in_specs=[pl.BlockSpec((rows, dm), lambda l, j: (0, 0)),
                  pl.BlockSpec((None, dm, tn), lambda l, j: (l, 0, j))],
        out_specs=pl.BlockSpec((None, rows, tn), lambda l, j: (l, 0, j)),
        out_shape=jax.ShapeDtypeStruct((depth, rows, n), BF16),
        compiler_params=_cparams("parallel", "parallel"),
        name="xa_kv_proj",
    )(mem2, wkv)


def _block_diag(w):
    depth, h, n, _ = w.shape
    eye = jnp.eye(h, dtype=w.dtype)
    return (w[:, :, :, None, :] * eye[None, :, None, :, None]).reshape(depth, h * n, h * n)


def _s5_discretise(lam_re, lam_im, log_dt, b_re, b_im, c_re, c_im):
    depth, g, p = lam_re.shape
    c = b_re.shape[-1]
    gb = LANES // c
    nq = g // gb
    dt = jnp.exp(log_dt)[..., None]
    mag = jnp.exp(lam_re * dt)
    a_re = mag * jnp.cos(lam_im * dt)
    a_im = mag * jnp.sin(lam_im * dt)
    den = lam_re * lam_re + lam_im * lam_im
    co_re = ((a_re - 1.0) * lam_re + a_im * lam_im) / den
    co_im = (a_im * lam_re - (a_re - 1.0) * lam_im) / den
    bb_re = co_re[..., None] * b_re - co_im[..., None] * b_im
    bb_im = co_re[..., None] * b_im + co_im[..., None] * b_re
    eye = jnp.eye(gb, dtype=F32)

    def b_layout(w):
        w = jnp.swapaxes(w.reshape(depth, nq, gb, p, c), 3, 4)
        return (w[:, :, :, :, None, :] * eye[None, None, :, None, :, None]).reshape(depth, nq, gb * c, gb * p)

    def c_layout(w):
        w = jnp.swapaxes(w.reshape(depth, nq, gb, c, p), 3, 4)
        return (w[:, :, :, :, None, :] * eye[None, None, :, None, :, None]).reshape(depth, nq, gb * p, gb * c)

    bq = jnp.concatenate([b_layout(bb_re), b_layout(bb_im)], axis=-1).astype(BF16)
    cq = jnp.concatenate([c_layout(c_re), c_layout(-c_im)], axis=-2).astype(BF16)
    rows8 = lambda a: jnp.broadcast_to(a.reshape(depth, 1, g * p), (depth, SUBLANES, g * p))
    return bq, cq, rows8(a_re), rows8(a_im)


def _tile(n, target):
    return min(n, target)


def kernel(x, mem, ffn1_wg, ffn1_wu, ffn1_wd, ln1_g, ln1_b, w_in, s5_lam_re, s5_lam_im, s5_log_dt, s5_b_re, s5_b_im, s5_c_re, s5_c_im, s5_d, s5_glu_w1, s5_glu_w2, lru_conv_w, lru_conv_b, lru_wa, lru_ba, lru_wx, lru_bx, lru_lambda, lru_w_out, rw_mu, rw_w0, rw_w_up, rw_a0, rw_a_up, rw_g_up, rw_k_k, rw_k_a, rw_r_k, rw_ln_g, rw_ln_b, rw_w_out, mix_w_out, ln2_g, ln2_b, xa_wq, xa_wkv, xa_wo, ln3_g, ln3_b, ffn2_wg, ffn2_wu, ffn2_wd, ln4_g, ln4_b):
    bsz, seq, dm = x.shape
    depth = w_in.shape[0]
    wdt = s5_d.shape[1]
    heads = wdt // HEAD_DIM
    alpha = (2 * depth) ** 0.25
    assert bsz == SUBLANES, "one time step must be one aligned 8-row group"
    t = bsz * seq

    bf = lambda w: w.astype(BF16)
    vec = lambda p: p.reshape(depth, 1, -1)

    w_in_b = bf(w_in)
    w_s5, w_lx, w_ly = w_in_b[:, :, 0:wdt], w_in_b[:, :, wdt:2 * wdt], w_in_b[:, :, 2 * wdt:3 * wdt]
    n_rw = rw_mu.shape[1]
    w_rw = w_in_b[:, :, 3 * wdt:3 * wdt + n_rw]
    w_gate = w_in_b[:, :, 3 * wdt + n_rw:]
    ffn1 = (bf(ffn1_wg), bf(ffn1_wu), bf(ffn1_wd), vec(ln1_g), vec(ln1_b))
    ffn2 = (bf(ffn2_wg), bf(ffn2_wu), bf(ffn2_wd), vec(ln4_g), vec(ln4_b))

    bq, cq, a_re, a_im = _s5_discretise(s5_lam_re, s5_lam_im, s5_log_dt, s5_b_re, s5_b_im, s5_c_re, s5_c_im)
    wa_bd, wx_bd = bf(_block_diag(lru_wa)), bf(_block_diag(lru_wx))

    kh_n = LANES // (bsz * heads)
    assert kh_n == 2 and HEAD_DIM % kh_n == 0, "lane layout b*16 + kh*8 + h needs batch * heads * 2 == 128"
    kp_n = HEAD_DIM // kh_n
    key_perm = np.array([h_ * HEAD_DIM + kh_ * kp_n + kp_
                         for kp_ in range(kp_n) for kh_ in range(kh_n) for h_ in range(heads)])
    val_perm = np.array([h_ * HEAD_DIM + v_ for v_ in range(HEAD_DIM) for h_ in range(heads)])

    def rw_layout(p):
        return jnp.concatenate([p[..., 0:wdt][..., key_perm], p[..., wdt:2 * wdt][..., key_perm],
                                p[..., 2 * wdt:3 * wdt][..., val_perm],
                                p[..., 3 * wdt:]], axis=-1)

    w_rw_p = rw_layout(w_rw)
    mu_p = vec(rw_layout(rw_mu))
    zeros_dr = jnp.zeros((depth, RW_A_RANK, wdt), BF16)
    zeros_ar = jnp.zeros((depth, RW_DECAY_RANK, wdt), BF16)
    wup_pad = jnp.concatenate([bf(rw_w_up)[:, :, key_perm], zeros_dr], axis=1)
    aup_pad = jnp.concatenate([zeros_ar, bf(rw_a_up)[:, :, key_perm]], axis=1)
    gup_p = bf(rw_g_up)[:, :, val_perm]
    w0_p, a0_p = vec(rw_w0[:, key_perm]), vec(rw_a0[:, key_perm])
    kk_p, ka_p = vec(rw_k_k[:, key_perm]), vec(rw_k_a[:, key_perm])
    rk_p = vec(rw_r_k.reshape(depth, wdt)[:, key_perm])
    gn_g_p, gn_b_p = vec(rw_ln_g[:, val_perm]), vec(rw_ln_b[:, val_perm])
    rw_o = bf(rw_w_out)[:, val_perm, :]

    kv = _kv_proj(mem.reshape(-1, dm), bf(xa_wkv), 512)
    wq_b, wo_b = bf(xa_wq), bf(xa_wo)
    glu1, glu2, lru_o, mix = bf(s5_glu_w1), bf(s5_glu_w2), bf(lru_w_out), bf(mix_w_out)

    tm = _tile(seq, 512)

    h = x
    for l in range(depth):
        h, hb = _ffn_ln(h, *ffn1, l, alpha, tm, bsz, batch_major_in=(l == 0), with_bf16_copy=True)

        h_rows = hb.reshape(t, dm)
        y_s5 = _s5_branch(h_rows, w_s5, bq, cq, a_re, a_im, vec(s5_d), l, _tile(t, 512))
        y_lru = _lru_branch(h_rows, w_lx, w_ly, lru_conv_w, vec(lru_conv_b), wa_bd, vec(lru_ba), wx_bd,
                            vec(lru_bx), vec(lru_lambda), l, _tile(t, 1024))
        k5, vs, g, bonus = _rw_prep(hb, w_rw_p, mu_p, w0_p, wup_pad, a0_p, aup_pad, gup_p, kk_p, ka_p, rk_p,
                                    l, bsz, _tile(seq, 512))
        y_rw = _rw_scan(k5, vs, bsz, heads, _tile(seq, 64))

        h = _merge_xattn(h, y_s5.reshape(seq, bsz * wdt), y_lru.reshape(seq, bsz * wdt), y_rw, bonus, g,
                         w_gate, glu1, glu2, lru_o, rw_o, mix, gn_g_p, gn_b_p, vec(ln2_g), vec(ln2_b),
                         kv, wq_b, wo_b, vec(ln3_g), vec(ln3_b), l, alpha, tm, bsz)
        h = _ffn_ln(h, *ffn2, l, alpha, tm, bsz, batch_major_out=(l == depth - 1))
    return h
```

```python
import functools

import jax
import jax.numpy as jnp
import numpy as np
from jax import lax
from jax.experimental import pallas as pl
from jax.experimental.pallas import tpu as pltpu

F32 = jnp.float32
BF16 = jnp.bfloat16

LN_EPS = 1e-5
RW_GN_EPS = 64e-5
LRU_C = 8.0
LRU_CONV = 4
HEAD_DIM = 64
XA_HEADS = 4
RW_DECAY_RANK = 64
RW_A_RANK = 64

SUBLANES = 8
LANES = 128
VMEM_LIMIT_BYTES = 56 * 1024 * 1024
RELAYOUT_ROWS = 64


def _cparams(*sem):
    return pltpu.CompilerParams(dimension_semantics=sem, vmem_limit_bytes=VMEM_LIMIT_BYTES)


def _layer_spec(arr, l):
    tail = arr.shape[1:]
    nd = len(tail)
    return pl.BlockSpec((None,) + tail, lambda *_: (l,) + (0,) * nd, pipeline_mode=pl.Buffered(1))


def _layer_norm(y, g, b):
    mu = jnp.mean(y, -1, keepdims=True)
    yc = y - mu
    var = jnp.mean(yc * yc, -1, keepdims=True)
    return yc * lax.rsqrt(var + LN_EPS) * g + b


def _mm(a, b):
    return jnp.dot(a, b, preferred_element_type=F32)


def _head_sums(x, heads):
    blocks = x.shape[1] // LANES
    acc = x[:, 0:LANES]
    for j in range(1, blocks):
        acc = acc + x[:, j * LANES:(j + 1) * LANES]
    shift = heads
    while shift < LANES:
        acc = acc + pltpu.roll(acc, shift, 1)
        shift *= 2
    return jnp.concatenate([acc] * blocks, axis=1)


def _softplus(x):
    return jnp.maximum(x, 0.0) + jnp.log1p(jnp.exp(-jnp.abs(x)))


def _ffn_kernel(x_ref, wg_ref, wu_ref, wd_ref, g_ref, b_ref, o_ref, *maybe_ob_ref, alpha, f_chunk):
    x = x_ref[...]
    xb = x.astype(BF16)
    d_ff = wg_ref.shape[1]
    acc = jnp.zeros(x.shape, F32)
    for c0 in range(0, d_ff, f_chunk):
        hg = _mm(xb, wg_ref[:, c0:c0 + f_chunk])
        hu = _mm(xb, wu_ref[:, c0:c0 + f_chunk])
        h = hg * jax.nn.sigmoid(hg) * hu
        acc = acc + _mm(h.astype(BF16), wd_ref[c0:c0 + f_chunk, :])
    out = _layer_norm(alpha * x + 0.5 * acc, g_ref[...], b_ref[...])
    o_ref[...] = out
    for ob_ref in maybe_ob_ref:
        ob_ref[...] = out.astype(ob_ref.dtype)


def _seq_spec(tm, n):
    return pl.BlockSpec((tm, n), lambda i, b: (i, b))


def _batch_major_spec(tm, n):
    return pl.BlockSpec((None, tm, n), lambda i, b: (b, i, 0))


def _ffn_ln(x, wg, wu, wd, g, b, l, alpha, tm, bsz, batch_major_in=False, batch_major_out=False,
            with_bf16_copy=False):
    d = wg.shape[1]
    seq = x.shape[1] if batch_major_in else x.shape[0]
    out_shape = (bsz, seq, d) if batch_major_out else (seq, bsz * d)
    out_spec = _batch_major_spec(tm, d) if batch_major_out else _seq_spec(tm, d)
    out_specs, out_shapes = out_spec, jax.ShapeDtypeStruct(out_shape, F32)
    if with_bf16_copy:
        out_specs = [out_spec, out_spec]
        out_shapes = [out_shapes, jax.ShapeDtypeStruct(out_shape, BF16)]
    return pl.pallas_call(
        functools.partial(_ffn_kernel, alpha=alpha, f_chunk=256),
        grid=(seq // tm, bsz),
        in_specs=[_batch_major_spec(tm, d) if batch_major_in else _seq_spec(tm, d),
                  _layer_spec(wg, l), _layer_spec(wu, l), _layer_spec(wd, l),
                  _layer_spec(g, l), _layer_spec(b, l)],
        out_specs=out_specs,
        out_shape=out_shapes,
        compiler_params=_cparams("parallel", "parallel"),
        name="ffn_ln",
    )(x, wg, wu, wd, g, b)


def _s5_kernel(x_ref, w_ref, bq_ref, cq_ref, are_ref, aim_ref, d_ref, o_ref,
               st_ref, carry_ref, u_ref):
    rows = x_ref.shape[0]
    nq = bq_ref.shape[0]
    half = bq_ref.shape[2] // 2
    cw = bq_ref.shape[1]

    @pl.when(pl.program_id(0) == 0)
    def _():
        carry_ref[...] = jnp.zeros_like(carry_ref)

    u = _mm(x_ref[...].astype(BF16), w_ref[...])
    u_ref[...] = u
    ub = u.astype(BF16)
    for q in range(nq):
        st_ref[:, 2 * half * q:2 * half * (q + 1)] = _mm(ub[:, cw * q:cw * (q + 1)], bq_ref[q])

    def step(t, s):
        r0 = pl.multiple_of(t * SUBLANES, SUBLANES)
        parts = []
        for q in range(nq):
            re = s[:, 2 * half * q:2 * half * q + half]
            im = s[:, 2 * half * q + half:2 * half * (q + 1)]
            ar = are_ref[:, half * q:half * (q + 1)]
            ai = aim_ref[:, half * q:half * (q + 1)]
            bre = st_ref[pl.ds(r0, SUBLANES), 2 * half * q:2 * half * q + half]
            bim = st_ref[pl.ds(r0, SUBLANES), 2 * half * q + half:2 * half * (q + 1)]
            parts.append(ar * re - ai * im + bre)
            parts.append(ar * im + ai * re + bim)
        s_new = jnp.concatenate(parts, axis=1)
        st_ref[pl.ds(r0, SUBLANES), :] = s_new
        return s_new

    carry_ref[...] = lax.fori_loop(0, rows // SUBLANES, step, carry_ref[...])

    for q in range(nq):
        y = _mm(st_ref[:, 2 * half * q:2 * half * (q + 1)].astype(BF16), cq_ref[q])
        y = y + d_ref[:, cw * q:cw * (q + 1)] * u_ref[:, cw * q:cw * (q + 1)]
        o_ref[:, cw * q:cw * (q + 1)] = jax.nn.gelu(y).astype(o_ref.dtype)


def _s5_branch(x, w_s5, bq, cq, a_re, a_im, d, l, rows):
    t, dm = x.shape
    wdt = w_s5.shape[2]
    n_state = a_re.shape[2]
    return pl.pallas_call(
        _s5_kernel,
        grid=(t // rows,),
        in_specs=[pl.BlockSpec((rows, dm), lambda i: (i, 0)),
                  _layer_spec(w_s5, l), _layer_spec(bq, l), _layer_spec(cq, l),
                  _layer_spec(a_re, l), _layer_spec(a_im, l), _layer_spec(d, l)],
        out_specs=pl.BlockSpec((rows, wdt), lambda i: (i, 0)),
        out_shape=jax.ShapeDtypeStruct((t, wdt), BF16),
        scratch_shapes=[pltpu.VMEM((rows, 2 * n_state), F32),
                        pltpu.VMEM((SUBLANES, 2 * n_state), F32),
                        pltpu.VMEM((rows, wdt), F32)],
        compiler_params=_cparams("arbitrary"),
        name="s5_branch",
    )(x, w_s5, bq, cq, a_re, a_im, d)


def _lru_kernel(x_ref, wlx_ref, wly_ref, cw_ref, cb_ref, wa_ref, ba_ref, wx_ref, bx_ref,
                lam_ref, o_ref, hist_ref, h_ref, a_ref, b_ref):
    rows = x_ref.shape[0]
    hist_rows = hist_ref.shape[0]

    @pl.when(pl.program_id(0) == 0)
    def _():
        hist_ref[...] = jnp.zeros_like(hist_ref)
        h_ref[...] = jnp.zeros_like(h_ref)

    xb = x_ref[...].astype(BF16)
    zx = _mm(xb, wlx_ref[...])
    zy = _mm(xb, wly_ref[...])

    ext = jnp.concatenate([hist_ref[...], zx], axis=0)
    hist_ref[...] = zx[rows - hist_rows:, :]
    xc = cb_ref[...] + cw_ref[LRU_CONV - 1:LRU_CONV, :] * zx
    for j in range(1, LRU_CONV):
        off = hist_rows - SUBLANES * j
        xc = xc + cw_ref[LRU_CONV - 1 - j:LRU_CONV - j, :] * ext[off:off + rows, :]

    xcb = xc.astype(BF16)
    gate_r = jax.nn.sigmoid(_mm(xcb, wa_ref[...]) + ba_ref[...])
    gate_i = jax.nn.sigmoid(_mm(xcb, wx_ref[...]) + bx_ref[...])
    log_a = -LRU_C * gate_r * _softplus(-lam_ref[...])
    a = jnp.exp(log_a)
    a_ref[...] = a
    b_ref[...] = jnp.sqrt(-jnp.tanh(log_a) * (1.0 + a * a)) * gate_i * xc

    def step(t, h):
        r0 = pl.multiple_of(t * SUBLANES, SUBLANES)
        h = a_ref[pl.ds(r0, SUBLANES), :] * h + b_ref[pl.ds(r0, SUBLANES), :]
        b_ref[pl.ds(r0, SUBLANES), :] = h
        return h

    h_ref[...] = lax.fori_loop(0, rows // SUBLANES, step, h_ref[...], unroll=8)
    o_ref[...] = (b_ref[...] * jax.nn.gelu(zy)).astype(o_ref.dtype)


def _lru_branch(x, w_lx, w_ly, conv_w, conv_b, wa_bd, ba, wx_bd, bx, lam, l, rows):
    t, dm = x.shape
    wdt = w_lx.shape[2]
    params = (w_lx, w_ly, conv_w, conv_b, wa_bd, ba, wx_bd, bx, lam)
    return pl.pallas_call(
        _lru_kernel,
        grid=(t // rows,),
        in_specs=[pl.BlockSpec((rows, dm), lambda i: (i, 0))] + [_layer_spec(p, l) for p in params],
        out_specs=pl.BlockSpec((rows, wdt), lambda i: (i, 0)),
        out_shape=jax.ShapeDtypeStruct((t, wdt), BF16),
        scratch_shapes=[pltpu.VMEM(((LRU_CONV - 1) * SUBLANES, wdt), F32),
                        pltpu.VMEM((SUBLANES, wdt), F32),
                        pltpu.VMEM((rows, wdt), F32),
                        pltpu.VMEM((rows, wdt), F32)],
        compiler_params=_cparams("arbitrary"),
        name="lru_branch",
    )(x, *params)


def _swap_tile_and_lane_group(tiles, grp):
    n = len(tiles)
    lane_g = lax.broadcasted_iota(jnp.int32, tiles[0].shape, 1) // grp
    k = 1
    while k < n:
        high = (lane_g & k) != 0
        nxt = list(tiles)
        for p in range(n):
            if p & k == 0:
                lo_t, hi_t = tiles[p], tiles[p + k]
                nxt[p] = jnp.where(high, pltpu.roll(hi_t, k * grp, 1), lo_t)
                nxt[p + k] = jnp.where(high, hi_t, pltpu.roll(lo_t, LANES - k * grp, 1))
        tiles = nxt
        k *= 2
    return tiles


def _rw_prep_kernel(x_ref, w_ref, mu_ref, w0_ref, wup_ref, a0_ref, aup_ref, gup_ref,
                    kk_ref, ka_ref, rk_ref,
                    k5_ref, vs_ref, g_ref, bonus_ref, prev_ref):
    tt = x_ref.shape[0]
    cols = w_ref.shape[1]
    wdt = w0_ref.shape[1]
    lora0 = 3 * wdt
    lora1 = lora0 + wup_ref.shape[0]
    b = pl.program_id(1)

    @pl.when(pl.program_id(0) == 0)
    def _():
        prev_ref[b] = jnp.zeros(prev_ref.shape[1:], F32)

    z = _mm(x_ref[...].astype(BF16), w_ref[...])
    first_row = lax.broadcasted_iota(jnp.int32, (tt, cols), 0) == 0
    shifted = jnp.where(first_row, prev_ref[b, 0:1, :], pltpu.roll(z, 1, 0))
    prev_ref[b, 0:1, :] = z[tt - 1:tt, :]
    z = z + mu_ref[...] * (shifted - z)

    r = z[:, 0:wdt]
    k = z[:, wdt:2 * wdt]
    v = z[:, 2 * wdt:3 * wdt]
    lora = z[:, lora0:lora1]
    gd = z[:, lora1:]

    w_log = -_softplus(-(w0_ref[...] + _mm(jnp.tanh(lora).astype(BF16), wup_ref[...]))) - 0.5
    decay = jnp.exp(-jnp.exp(w_log))
    a = jax.nn.sigmoid(a0_ref[...] + _mm(lora.astype(BF16), aup_ref[...]))
    g_ref[...] = _mm(jax.nn.sigmoid(gd).astype(BF16), gup_ref[...])

    heads = wdt // HEAD_DIM
    kk = k * kk_ref[...]
    kk = kk * lax.rsqrt(_head_sums(kk * kk, heads) + 1e-12)
    k = k * (1.0 + (a - 1.0) * ka_ref[...])
    kka = kk * a

    k5_ref[:, 0:wdt] = -kk
    k5_ref[:, wdt:2 * wdt] = decay
    k5_ref[:, 2 * wdt:3 * wdt] = kka
    k5_ref[:, 3 * wdt:4 * wdt] = k
    k5_ref[:, 4 * wdt:5 * wdt] = decay * r
    vs_ref[:, 0:wdt] = v
    first_group = lax.broadcasted_iota(jnp.int32, (tt, LANES), 1) < LANES // prev_ref.shape[0]
    vs_ref[:, wdt:wdt + LANES] = jnp.where(first_group, _head_sums(kka * r, heads)[:, 0:LANES],
                                         _head_sums(k * r, heads)[:, 0:LANES])
    bonus_ref[...] = _head_sums(r * k * rk_ref[...], heads) * v


def _rw_prep(x, w_rw, mu, w0, wup_pad, a0, aup_pad, gup, k_k, k_a, r_k, l, bsz, tt):
    seq = x.shape[0]
    dm = x.shape[1] // bsz
    wdt = w0.shape[2]
    cols = w_rw.shape[2]
    params = (w_rw, mu, w0, wup_pad, a0, aup_pad, gup, k_k, k_a, r_k)
    per_batch = lambda n: _seq_spec(tt, n)
    widths = (5 * wdt, wdt + LANES, wdt, wdt)
    return pl.pallas_call(
        _rw_prep_kernel,
        grid=(seq // tt, bsz),
        in_specs=[per_batch(dm)] + [_layer_spec(p, l) for p in params],
        out_specs=[per_batch(n) for n in widths],
        out_shape=[jax.ShapeDtypeStruct((seq, bsz * n), F32) for n in widths],
        scratch_shapes=[pltpu.VMEM((bsz, SUBLANES, cols), F32)],
        compiler_params=_cparams("arbitrary", "arbitrary"),
        name="rw_prep",
    )(x, *params)


def _rw_scan_kernel(k5_ref, vs_ref, y_ref, st_ref, k5p_ref, v2x_ref, ypl_ref, *, bsz, heads):
    tt = k5_ref.shape[0]
    kp_n = st_ref.shape[0]
    hd = st_ref.shape[1]
    wdt = hd * heads
    grp = LANES // bsz
    k5w = k5_ref.shape[1] // bsz
    vsw = vs_ref.shape[1] // bsz
    vt = hd // SUBLANES
    @pl.when(pl.program_id(0) == 0)
    def _():
        st_ref[...] = jnp.zeros_like(st_ref)

    rc = min(tt, RELAYOUT_ROWS)

    def batch_tiles(src_ref, width, blk, r0):
        return [src_ref[r0:r0 + rc, pl.ds(pl.multiple_of(b * width + blk * LANES, LANES), LANES)]
                for b in range(bsz)]

    def k5_block(blk, carry):
        for r0 in range(0, tt, rc):
            for i, tile in enumerate(_swap_tile_and_lane_group(batch_tiles(k5_ref, k5w, blk, r0), grp)):
                k5p_ref[blk * bsz + i, r0:r0 + rc, :] = tile
        return carry

    pair_lo = lax.broadcasted_iota(jnp.int32, (rc, LANES), 1) % grp < heads

    def v_block(vb, carry):
        for r0 in range(0, tt, rc):
            pairs = _swap_tile_and_lane_group(batch_tiles(vs_ref, vsw, vb, r0), grp)
            for m, pair in enumerate(pairs):
                rows = (jnp.where(pair_lo, pair, pltpu.roll(pair, heads, 1)),
                        jnp.where(pair_lo, pltpu.roll(pair, LANES - heads, 1), pair))
                for par, row_tile in enumerate(rows):
                    j = 2 * m + par
                    v2x_ref[vb * 2 + j // SUBLANES, r0:r0 + rc, j % SUBLANES, :] = row_tile
        return carry

    lax.fori_loop(0, k5w // LANES, k5_block, 0, unroll=4)
    lax.fori_loop(0, vt // 2, v_block, 0, unroll=2)
    for r0 in range(0, tt, rc):
        dots = _swap_tile_and_lane_group(batch_tiles(vs_ref, vsw, vt // 2, r0), grp)
        for i in range(2):
            v2x_ref[vt, r0:r0 + rc, i, :] = dots[i]

    low_half = (lax.broadcasted_iota(jnp.int32, (hd, LANES), 1) // heads) % 2 == 0

    def fold(p):
        return p + jnp.where(low_half, pltpu.roll(p, LANES - heads, 1), pltpu.roll(p, heads, 1))

    def row(j, kp, t):
        return k5p_ref[j * kp_n + kp, pl.ds(t, 1), :]

    def step(t, carry):
        vv = jnp.concatenate([v2x_ref[m, t] for m in range(vt)], axis=0)
        b_r = v2x_ref[vt, t, 0:1, :]
        k_r = v2x_ref[vt, t, 1:2, :]
        sa = jnp.zeros((hd, LANES), F32)
        yp = jnp.zeros((hd, LANES), F32)
        for kp in range(kp_n):
            s_k = st_ref[kp]
            sa = sa + s_k * row(0, kp, t)
            yp = yp + s_k * row(4, kp, t)
        sa = fold(sa)
        yp = fold(yp)
        for kp in range(kp_n):
            st_ref[kp] = st_ref[kp] * row(1, kp, t) + sa * row(2, kp, t) + vv * row(3, kp, t)
        y = yp + sa * b_r + vv * k_r
        for v in range(hd):
            ypl_ref[v, pl.ds(t, 1), :] = y[v:v + 1, :]
        return carry

    lax.fori_loop(0, tt, step, 0)


    def out_block(vb, carry):
        for r0 in range(0, tt, rc):
            pairs = []
            for m in range(bsz):
                even = ypl_ref[vb * 2 * bsz + 2 * m, r0:r0 + rc, :]
                odd = ypl_ref[vb * 2 * bsz + 2 * m + 1, r0:r0 + rc, :]
                pairs.append(jnp.where(pair_lo, even, pltpu.roll(odd, heads, 1)))
            for b, tile in enumerate(_swap_tile_and_lane_group(pairs, grp)):
                y_ref[r0:r0 + rc, pl.ds(pl.multiple_of(b * wdt + vb * LANES, LANES), LANES)] = tile
        return carry

    lax.fori_loop(0, wdt // LANES, out_block, 0, unroll=2)


def _rw_scan(k5, vs, bsz, heads, tt):
    seq = k5.shape[0]
    hd = HEAD_DIM
    wdt = hd * heads
    n_k5 = k5.shape[1] // LANES
    n_vs = hd // SUBLANES + 1
    return pl.pallas_call(
        functools.partial(_rw_scan_kernel, bsz=bsz, heads=heads),
        grid=(seq // tt,),
        in_specs=[pl.BlockSpec((tt, k5.shape[1]), lambda i: (i, 0)),
                  pl.BlockSpec((tt, vs.shape[1]), lambda i: (i, 0))],
        out_specs=pl.BlockSpec((tt, bsz * wdt), lambda i: (i, 0)),
        out_shape=jax.ShapeDtypeStruct((seq, bsz * wdt), F32),
        scratch_shapes=[pltpu.VMEM((n_k5 // 5, hd, LANES), F32),
                        pltpu.VMEM((n_k5, tt, LANES), F32),
                        pltpu.VMEM((n_vs, tt, bsz, LANES), F32),
                        pltpu.VMEM((hd, tt, LANES), F32)],
        compiler_params=_cparams("arbitrary"),
        name="rw_scan",
    )(k5, vs)


def _xattn_rows(x, k_ref, v_ref, wq_ref, wo_ref, heads):
    dm = x.shape[1]
    hd = dm // heads
    q = _mm(x.astype(BF16), wq_ref[...]).astype(BF16)
    outs = []
    for h in range(heads):
        sl = slice(h * hd, (h + 1) * hd)
        s = lax.dot_general(q[:, sl], k_ref[:, sl], (((1,), (1,)), ((), ())),
                            preferred_element_type=F32) * (hd ** -0.5)
        e = jnp.exp(s - jnp.max(s, -1, keepdims=True))
        p = e / jnp.sum(e, -1, keepdims=True)
        outs.append(_mm(p.astype(BF16), v_ref[:, sl]))
    return _mm(jnp.concatenate(outs, axis=1).astype(BF16), wo_ref[...])


def _merge_xattn_kernel(x_ref, ys5_ref, ylru_ref, yrw_ref, bonus_ref, g_ref,
                        wgate_ref, glu1_ref, glu2_ref, lruo_ref, rwo_ref, mix_ref,
                        gng_ref, gnb_ref, ln2g_ref, ln2b_ref,
                        k_ref, v_ref, wq_ref, wo_ref, ln3g_ref, ln3b_ref, o_ref, *, alpha, xa_heads):
    x = x_ref[...]
    dm = x.shape[1]
    gates = jax.nn.sigmoid(_mm(x.astype(BF16), wgate_ref[...]))

    ys5 = ys5_ref[...].astype(BF16)
    y_s5 = _mm(ys5, glu1_ref[...]) * jax.nn.sigmoid(_mm(ys5, glu2_ref[...]))
    y_lru = _mm(ylru_ref[...].astype(BF16), lruo_ref[...])

    inv_n = 1.0 / HEAD_DIM
    y = yrw_ref[...]
    heads = y.shape[1] // HEAD_DIM
    yc = y - _head_sums(y, heads) * inv_n
    var = _head_sums(yc * yc, heads) * inv_n
    y = yc * lax.rsqrt(var + RW_GN_EPS) * gng_ref[...] + gnb_ref[...]
    y = (y + bonus_ref[...]) * g_ref[...]
    y_rw = _mm(y.astype(BF16), rwo_ref[...])

    merged = (gates[:, 0:dm] * y_s5 + gates[:, dm:2 * dm] * y_lru + gates[:, 2 * dm:3 * dm] * y_rw)
    h = _layer_norm(alpha * x + _mm(merged.astype(BF16), mix_ref[...]), ln2g_ref[...], ln2b_ref[...])
    o_ref[...] = _layer_norm(alpha * h + _xattn_rows(h, k_ref, v_ref, wq_ref, wo_ref, xa_heads),
                             ln3g_ref[...], ln3b_ref[...])


def _merge_xattn(x, ys5, ylru, yrw, bonus, g, w_gate, glu1, glu2, lru_o, rw_o, mix, gn_g, gn_b,
                 ln2_g, ln2_b, kv, wq, wo, ln3_g, ln3_b, l, alpha, tm, bsz):
    seq = x.shape[0]
    dm = x.shape[1] // bsz
    wdt = ys5.shape[1] // bsz
    m = kv.shape[1] // bsz
    rows = lambda n: pl.BlockSpec((tm, n), lambda b, i: (i, b))
    merge_params = (w_gate, glu1, glu2, lru_o, rw_o, mix, gn_g, gn_b, ln2_g, ln2_b)
    xa_params = (wq, wo, ln3_g, ln3_b)
    return pl.pallas_call(
        functools.partial(_merge_xattn_kernel, alpha=alpha, xa_heads=XA_HEADS),
        grid=(bsz, seq // tm),
        in_specs=[rows(dm)] + [rows(wdt)] * 5 + [_layer_spec(p, l) for p in merge_params]
                 + [pl.BlockSpec((None, m, dm), lambda b, i: (l, b, 0)),
                    pl.BlockSpec((None, m, dm), lambda b, i: (l, b, 1))]
                 + [_layer_spec(p, l) for p in xa_params],
        out_specs=rows(dm),
        out_shape=jax.ShapeDtypeStruct((seq, bsz * dm), F32),
        compiler_params=_cparams("parallel", "parallel"),
        name="merge_xattn_ln",
    )(x, ys5, ylru, yrw, bonus, g, *merge_params, kv, kv, *xa_params)


def _kv_kernel(mem_ref, w_ref, o_ref):
    o_ref[...] = _mm(mem_ref[...].astype(BF16), w_ref[...]).astype(BF16)


def _kv_proj(mem2, wkv, tn):
    rows, dm = mem2.shape
    depth, _, n = wkv.shape
    return pl.pallas_call(
        _kv_kernel,
        grid=(depth, n // tn),
        in_specs=[pl.BlockSpec((rows, dm), lambda l, j: (0, 0)),
                  pl.BlockSpec((None, dm, tn), lambda l, j: (l, 0, j))],
        out_specs=pl.BlockSpec((None, rows, tn), lambda l, j: (l, 0, j)),
        out_shape=jax.ShapeDtypeStruct((depth, rows, n), BF16),
        compiler_params=_cparams("parallel", "parallel"),
        name="xa_kv_proj",
    )(mem2, wkv)


def _block_diag(w):
    depth, h, n, _ = w.shape
    eye = jnp.eye(h, dtype=w.dtype)
    return (w[:, :, :, None, :] * eye[None, :, None, :, None]).reshape(depth, h * n, h * n)


def _s5_discretise(lam_re, lam_im, log_dt, b_re, b_im, c_re, c_im):
    depth, g, p = lam_re.shape
    c = b_re.shape[-1]
    gb = LANES // c
    nq = g // gb
    dt = jnp.exp(log_dt)[..., None]
    mag = jnp.exp(lam_re * dt)
    a_re = mag * jnp.cos(lam_im * dt)
    a_im = mag * jnp.sin(lam_im * dt)
    den = lam_re * lam_re + lam_im * lam_im
    co_re = ((a_re - 1.0) * lam_re + a_im * lam_im) / den
    co_im = (a_im * lam_re - (a_re - 1.0) * lam_im) / den
    bb_re = co_re[..., None] * b_re - co_im[..., None] * b_im
    bb_im = co_re[..., None] * b_im + co_im[..., None] * b_re
    eye = jnp.eye(gb, dtype=F32)

    def b_layout(w):
        w = jnp.swapaxes(w.reshape(depth, nq, gb, p, c), 3, 4)
        return (w[:, :, :, :, None, :] * eye[None, None, :, None, :, None]).reshape(depth, nq, gb * c, gb * p)

    def c_layout(w):
        w = jnp.swapaxes(w.reshape(depth, nq, gb, c, p), 3, 4)
        return (w[:, :, :, :, None, :] * eye[None, None, :, None, :, None]).reshape(depth, nq, gb * p, gb * c)

    bq = jnp.concatenate([b_layout(bb_re), b_layout(bb_im)], axis=-1).astype(BF16)
    cq = jnp.concatenate([c_layout(c_re), c_layout(-c_im)], axis=-2).astype(BF16)
    rows8 = lambda a: jnp.broadcast_to(a.reshape(depth, 1, g * p), (depth, SUBLANES, g * p))
    return bq, cq, rows8(a_re), rows8(a_im)


def _tile(n, target):
    return min(n, target)


def kernel(x, mem, ffn1_wg, ffn1_wu, ffn1_wd, ln1_g, ln1_b, w_in, s5_lam_re, s5_lam_im, s5_log_dt, s5_b_re, s5_b_im, s5_c_re, s5_c_im, s5_d, s5_glu_w1, s5_glu_w2, lru_conv_w, lru_conv_b, lru_wa, lru_ba, lru_wx, lru_bx, lru_lambda, lru_w_out, rw_mu, rw_w0, rw_w_up, rw_a0, rw_a_up, rw_g_up, rw_k_k, rw_k_a, rw_r_k, rw_ln_g, rw_ln_b, rw_w_out, mix_w_out, ln2_g, ln2_b, xa_wq, xa_wkv, xa_wo, ln3_g, ln3_b, ffn2_wg, ffn2_wu, ffn2_wd, ln4_g, ln4_b):
    bsz, seq, dm = x.shape
    depth = w_in.shape[0]
    wdt = s5_d.shape[1]
    heads = wdt // HEAD_DIM
    alpha = (2 * depth) ** 0.25
    assert bsz == SUBLANES, "one time step must be one aligned 8-row group"
    t = bsz * seq

    bf = lambda w: w.astype(BF16)
    vec = lambda p: p.reshape(depth, 1, -1)

    w_in_b = bf(w_in)
    w_s5, w_lx, w_ly = w_in_b[:, :, 0:wdt], w_in_b[:, :, wdt:2 * wdt], w_in_b[:, :, 2 * wdt:3 * wdt]
    n_rw = rw_mu.shape[1]
    w_rw = w_in_b[:, :, 3 * wdt:3 * wdt + n_rw]
    w_gate = w_in_b[:, :, 3 * wdt + n_rw:]
    ffn1 = (bf(ffn1_wg), bf(ffn1_wu), bf(ffn1_wd), vec(ln1_g), vec(ln1_b))
    ffn2 = (bf(ffn2_wg), bf(ffn2_wu), bf(ffn2_wd), vec(ln4_g), vec(ln4_b))

    bq, cq, a_re, a_im = _s5_discretise(s5_lam_re, s5_lam_im, s5_log_dt, s5_b_re, s5_b_im, s5_c_re, s5_c_im)
    wa_bd, wx_bd = bf(_block_diag(lru_wa)), bf(_block_diag(lru_wx))

    kh_n = LANES // (bsz * heads)
    assert kh_n == 2 and HEAD_DIM % kh_n == 0, "lane layout b*16 + kh*8 + h needs batch * heads * 2 == 128"
    kp_n = HEAD_DIM // kh_n
    key_perm = np.array([h_ * HEAD_DIM + kh_ * kp_n + kp_
                         for kp_ in range(kp_n) for kh_ in range(kh_n) for h_ in range(heads)])
    val_perm = np.array([h_ * HEAD_DIM + v_ for v_ in range(HEAD_DIM) for h_ in range(heads)])

    def rw_layout(p):
        return jnp.concatenate([p[..., 0:wdt][..., key_perm], p[..., wdt:2 * wdt][..., key_perm],
                                p[..., 2 * wdt:3 * wdt][..., val_perm],
                                p[..., 3 * wdt:]], axis=-1)

    w_rw_p = rw_layout(w_rw)
    mu_p = vec(rw_layout(rw_mu))
    zeros_dr = jnp.zeros((depth, RW_A_RANK, wdt), BF16)
    zeros_ar = jnp.zeros((depth, RW_DECAY_RANK, wdt), BF16)
    wup_pad = jnp.concatenate([bf(rw_w_up)[:, :, key_perm], zeros_dr], axis=1)
    aup_pad = jnp.concatenate([zeros_ar, bf(rw_a_up)[:, :, key_perm]], axis=1)
    gup_p = bf(rw_g_up)[:, :, val_perm]
    w0_p, a0_p = vec(rw_w0[:, key_perm]), vec(rw_a0[:, key_perm])
    kk_p, ka_p = vec(rw_k_k[:, key_perm]), vec(rw_k_a[:, key_perm])
    rk_p = vec(rw_r_k.reshape(depth, wdt)[:, key_perm])
    gn_g_p, gn_b_p = vec(rw_ln_g[:, val_perm]), vec(rw_ln_b[:, val_perm])
    rw_o = bf(rw_w_out)[:, val_perm, :]

    kv = _kv_proj(mem.reshape(-1, dm), bf(xa_wkv), 512)
    wq_b, wo_b = bf(xa_wq), bf(xa_wo)
    glu1, glu2, lru_o, mix = bf(s5_glu_w1), bf(s5_glu_w2), bf(lru_w_out), bf(mix_w_out)

    tm = _tile(seq, 512)
    tm_ffn = _tile(seq, 1024)

    h = x
    for l in range(depth):
        h, hb = _ffn_ln(h, *ffn1, l, alpha, tm_ffn, bsz, batch_major_in=(l == 0), with_bf16_copy=True)

        h_rows = hb.reshape(t, dm)
        y_s5 = _s5_branch(h_rows, w_s5, bq, cq, a_re, a_im, vec(s5_d), l, _tile(t, 1024))
        y_lru = _lru_branch(h_rows, w_lx, w_ly, lru_conv_w, vec(lru_conv_b), wa_bd, vec(lru_ba), wx_bd,
                            vec(lru_bx), vec(lru_lambda), l, _tile(t, 1024))
        k5, vs, g, bonus = _rw_prep(hb, w_rw_p, mu_p, w0_p, wup_pad, a0_p, aup_pad, gup_p, kk_p, ka_p, rk_p,
                                    l, bsz, _tile(seq, 512))
        y_rw = _rw_scan(k5, vs, bsz, heads, _tile(seq, 64))

        h = _merge_xattn(h, y_s5.reshape(seq, bsz * wdt), y_lru.reshape(seq, bsz * wdt), y_rw, bonus, g,
                         w_gate, glu1, glu2, lru_o, rw_o, mix, gn_g_p, gn_b_p, vec(ln2_g), vec(ln2_b),
                         kv, wq_b, wo_b, vec(ln3_g), vec(ln3_b), l, alpha, tm, bsz)
        h = _ffn_ln(h, *ffn2, l, alpha, tm_ffn, bsz, batch_major_out=(l == depth - 1))
    return h
```
